```python
import math
import jax, jax.numpy as jnp
from jax import lax
import numpy as np

D_MODEL = 2048
BATCH = 1
SEQ = 8192
DEPTH = 1
DEC_BATCH = 32
DEC_SEQ = 4
PAST_LEN = 16384
PAGE_SIZE = 128

D_HEAD = 128
H_SB = D_MODEL // (2 * D_HEAD)
H_DIFF = D_MODEL // (4 * D_HEAD)
D_V_DIFF = 2 * D_HEAD
H_MEM = 4
N_MEM = 256
NUM_BUCKETS = 32
MAX_DISTANCE = 128
D_FF = ((8 * D_MODEL // 3 + 127) // 128) * 128
CONV_W = 3
Q_BLOCK = 128
EPS = 1e-6
SCALE = D_HEAD ** -0.5
SB_W = H_SB * D_HEAD
DIFF_QK_W = H_DIFF * 2 * D_HEAD
DIFF_V_W = H_DIFF * D_V_DIFF
IN_SIZES = (SB_W, SB_W, SB_W, DIFF_QK_W, DIFF_QK_W, DIFF_V_W, D_MODEL, D_MODEL)
D_IN = SB_W * 3 + DIFF_QK_W * 2 + DIFF_V_W + D_MODEL * 2

kernel_name = "stickbreak_diffattn_hybrid_step"

F32 = jnp.float32


def rms_norm(x, g):
    xf = x.astype(F32)
    y = xf * lax.rsqrt(jnp.mean(xf * xf, axis=-1, keepdims=True) + EPS)
    return (y * g.astype(F32)).astype(x.dtype)


def t5_bucket(rel):
    n = jnp.maximum(rel, 0)
    max_exact = NUM_BUCKETS // 2
    nf = jnp.maximum(n, 1).astype(F32)
    large = max_exact + (jnp.log(nf / max_exact) / math.log(MAX_DISTANCE / max_exact)
                         * (NUM_BUCKETS - max_exact)).astype(jnp.int32)
    large = jnp.minimum(large, NUM_BUCKETS - 1)
    return jnp.where(n < max_exact, n, large)


def stick_breaking(q, k, v, q_pos, k_pos):
    z = jnp.einsum('bqhd,bkhd->bhqk', q.astype(F32), k.astype(F32)) * SCALE
    visible = (k_pos[None, :] < q_pos[:, None])[None, None]
    log_beta = jnp.where(visible, jax.nn.log_sigmoid(z), -jnp.inf)
    log_keep = jnp.where(visible, jax.nn.log_sigmoid(-z), 0.0)
    later = lax.cumsum(log_keep, axis=3, reverse=True) - log_keep
    a = jnp.exp(log_beta + later)
    o = jnp.einsum('bhqk,bkhd->bqhd', a, v.astype(F32))
    return o.astype(v.dtype)


def diff_attention(q, k, v, q_pos, k_pos, bias_table, lam):
    s = jnp.einsum('bqhcd,bkhcd->bchqk', q.astype(F32), k.astype(F32)) * SCALE
    rel = q_pos[:, None] - k_pos[None, :]
    bias = jnp.moveaxis(bias_table.astype(F32)[t5_bucket(rel)], -1, 0)
    s = jnp.where(rel >= 0, s + bias, -jnp.inf)
    p = jax.nn.softmax(s, axis=-1)
    w = p[:, 0] - lam * p[:, 1]
    return jnp.einsum('bhqk,bkhd->bqhd', w, v.astype(F32)).astype(v.dtype)


def sweep_query_blocks(attend, q, q_pos):
    B, T = q.shape[0], q.shape[1]
    n_blk = T // Q_BLOCK
    qb = jnp.moveaxis(q.reshape((B, n_blk, Q_BLOCK) + q.shape[2:]), 1, 0)
    pb = q_pos.reshape(n_blk, Q_BLOCK)
    ob = lax.map(lambda a: attend(a[0], a[1]), (qb, pb))
    return jnp.moveaxis(ob, 0, 1).reshape((B, T) + ob.shape[3:])


def project_in(h, w_in):
    B, T, _ = h.shape
    q_sb, k_sb, v_sb, q_df, k_df, v_df, g_sb, g_df = jnp.split(
        h @ w_in, np.cumsum(IN_SIZES)[:-1].tolist(), axis=-1)
    return (q_sb.reshape(B, T, H_SB, D_HEAD), k_sb.reshape(B, T, H_SB, D_HEAD),
            v_sb.reshape(B, T, H_SB, D_HEAD),
            q_df.reshape(B, T, H_DIFF, 2, D_HEAD), k_df.reshape(B, T, H_DIFF, 2, D_HEAD),
            v_df.reshape(B, T, H_DIFF, D_V_DIFF),
            jax.nn.sigmoid(g_sb), jax.nn.sigmoid(g_df))


def merge_branches(o_sb, o_df, g_sb, g_df, subln_g, lam_init, w_sb_o, w_diff_o, w_out):
    B, T = o_sb.shape[0], o_sb.shape[1]
    o_df = rms_norm(o_df, subln_g) * (1.0 - lam_init)
    y_sb = o_sb.reshape(B, T, SB_W) @ w_sb_o
    y_df = o_df.reshape(B, T, DIFF_V_W) @ w_diff_o
    return (g_sb * y_sb + g_df * y_df) @ w_out


def memory_kv(mem, norm_mem, w_mk, w_mv):
    B, M, _ = mem.shape
    m = rms_norm(mem, norm_mem)
    return ((m @ w_mk).reshape(B, M, H_MEM, D_HEAD), (m @ w_mv).reshape(B, M, H_MEM, D_HEAD))


def cross_attention(h, mk, mv, w_mq, w_mo):
    B, T, _ = h.shape
    q = (h @ w_mq).reshape(B, T, H_MEM, D_HEAD)
    s = jnp.einsum('bqhd,bmhd->bhqm', q.astype(F32), mk.astype(F32)) * SCALE
    p = jax.nn.softmax(s, axis=-1)
    o = jnp.einsum('bhqm,bmhd->bqhd', p, mv.astype(F32)).astype(h.dtype)
    return o.reshape(B, T, H_MEM * D_HEAD) @ w_mo


def conv_ffn(h, prev, w_gate, w_up, conv_w, conv_b, w_down):
    T = h.shape[1]
    g = h @ w_gate
    u = h @ w_up
    gp = jnp.concatenate([prev.astype(g.dtype), g], axis=1)
    c = conv_b
    for i in range(CONV_W):
        c = c + conv_w[i] * gp[:, i:i + T]
    y = (jax.nn.silu(c) * u) @ w_down
    return y, gp[:, gp.shape[1] - (CONV_W - 1):]


def setup_inputs(seed: int = 0) -> dict:
    key = jax.random.key(seed)
    ks = jax.random.split(key, 48)
    n_pages = PAST_LEN // PAGE_SIZE
    n_pool = DEC_BATCH * n_pages * 5 // 4

    def nrm(k, shape, scale):
        return jax.random.normal(k, shape, F32) * scale

    def gain(k, shape):
        return 1.0 + 0.02 * jax.random.normal(k, shape, F32)

    page_table = jax.random.permutation(ks[0], n_pool)[:DEC_BATCH * n_pages]
    page_table = page_table.reshape(DEC_BATCH, n_pages).astype(jnp.int32)
    mw = H_MEM * D_HEAD
    return {
        "x_prompt": nrm(ks[1], (BATCH, SEQ, D_MODEL), 1.0),
        "x_sample": nrm(ks[2], (DEC_BATCH, DEC_SEQ, D_MODEL), 1.0),
        "mem_prompt": nrm(ks[3], (BATCH, N_MEM, D_MODEL), 1.0),
        "cache_sb_k": nrm(ks[4], (DEPTH, n_pool, PAGE_SIZE, H_SB, D_HEAD), 1.0),
        "cache_sb_v": nrm(ks[5], (DEPTH, n_pool, PAGE_SIZE, H_SB, D_HEAD), 1.0),
        "cache_diff_k": nrm(ks[6], (DEPTH, n_pool, PAGE_SIZE, H_DIFF, 2, D_HEAD), 1.0),
        "cache_diff_v": nrm(ks[7], (DEPTH, n_pool, PAGE_SIZE, H_DIFF, D_V_DIFF), 1.0),
        "cache_mem_k": nrm(ks[8], (DEPTH, DEC_BATCH, N_MEM, H_MEM, D_HEAD), 1.0),
        "cache_mem_v": nrm(ks[9], (DEPTH, DEC_BATCH, N_MEM, H_MEM, D_HEAD), 1.0),
        "state_conv": nrm(ks[10], (DEPTH, DEC_BATCH, CONV_W - 1, D_FF), 1.0),
        "page_table": page_table,
        "norm_mix": gain(ks[11], (DEPTH, D_MODEL)),
        "w_in": nrm(ks[12], (DEPTH, D_MODEL, D_IN), D_MODEL ** -0.5),
        "w_sb_o": nrm(ks[13], (DEPTH, SB_W, D_MODEL), SB_W ** -0.5),
        "w_diff_o": nrm(ks[14], (DEPTH, DIFF_V_W, D_MODEL), DIFF_V_W ** -0.5),
        "w_out": nrm(ks[15], (DEPTH, D_MODEL, D_MODEL), D_MODEL ** -0.5),
        "diff_subln_g": gain(ks[16], (DEPTH, D_V_DIFF)),
        "lambda_q1": nrm(ks[17], (DEPTH, D_HEAD), 0.1),
        "lambda_k1": nrm(ks[18], (DEPTH, D_HEAD), 0.1),
        "lambda_q2": nrm(ks[19], (DEPTH, D_HEAD), 0.1),
        "lambda_k2": nrm(ks[20], (DEPTH, D_HEAD), 0.1),
        "t5_bias": nrm(ks[21], (NUM_BUCKETS, H_DIFF), 0.5),
        "norm_cross": gain(ks[22], (DEPTH, D_MODEL)),
        "norm_mem": gain(ks[23], (DEPTH, D_MODEL)),
        "w_mq": nrm(ks[24], (DEPTH, D_MODEL, mw), D_MODEL ** -0.5),
        "w_mk": nrm(ks[25], (DEPTH, D_MODEL, mw), D_MODEL ** -0.5),
        "w_mv": nrm(ks[26], (DEPTH, D_MODEL, mw), D_MODEL ** -0.5),
        "w_mo": nrm(ks[27], (DEPTH, mw, D_MODEL), mw ** -0.5),
        "norm_ffn": gain(ks[28], (DEPTH, D_MODEL)),
        "w_gate": nrm(ks[29], (DEPTH, D_MODEL, D_FF), D_MODEL ** -0.5),
        "w_up": nrm(ks[30], (DEPTH, D_MODEL, D_FF), D_MODEL ** -0.5),
        "conv_w": nrm(ks[31], (DEPTH, CONV_W, D_FF), CONV_W ** -0.5),
        "conv_b": nrm(ks[32], (DEPTH, D_FF), 0.01),
        "w_down": nrm(ks[33], (DEPTH, D_FF, D_MODEL), D_FF ** -0.5),
        "norm_f": gain(ks[34], (D_MODEL,)),
    }


def reference(x_prompt, x_sample, mem_prompt, cache_sb_k, cache_sb_v, cache_diff_k, cache_diff_v,
              cache_mem_k, cache_mem_v, state_conv, page_table, norm_mix, w_in, w_sb_o, w_diff_o,
              w_out, diff_subln_g, lambda_q1, lambda_k1, lambda_q2, lambda_k2, t5_bias, norm_cross,
              norm_mem, w_mq, w_mk, w_mv, w_mo, norm_ffn, w_gate, w_up, conv_w, conv_b, w_down,
              norm_f):
    xp, xs = x_prompt, x_sample
    t_p, t_s = xp.shape[1], xs.shape[1]
    past_len = page_table.shape[1] * cache_sb_k.shape[2]
    pos_p = jnp.arange(t_p, dtype=jnp.int32)
    pos_sq = past_len + jnp.arange(t_s, dtype=jnp.int32)
    pos_sk = jnp.arange(past_len + t_s, dtype=jnp.int32)
    zeros_conv = jnp.zeros((xp.shape[0], CONV_W - 1, D_FF), xp.dtype)

    p_sb_k, p_sb_v, p_df_k, p_df_v, p_mem_k, p_mem_v, p_conv = [], [], [], [], [], [], []
    s_sb_k, s_sb_v, s_df_k, s_df_v, s_conv = [], [], [], [], []

    for l in range(DEPTH):
        lam_init = 0.8 - 0.6 * math.exp(-0.3 * l)
        lam = (jnp.exp(jnp.sum(lambda_q1[l].astype(F32) * lambda_k1[l].astype(F32)))
               - jnp.exp(jnp.sum(lambda_q2[l].astype(F32) * lambda_k2[l].astype(F32))) + lam_init)

        h = rms_norm(xp, norm_mix[l])
        q_sb, k_sb, v_sb, q_df, k_df, v_df, g_sb, g_df = project_in(h, w_in[l])
        o_sb = sweep_query_blocks(
            lambda qb, pb: stick_breaking(qb, k_sb, v_sb, pb, pos_p), q_sb, pos_p)
        o_df = sweep_query_blocks(
            lambda qb, pb: diff_attention(qb, k_df, v_df, pb, pos_p, t5_bias, lam), q_df, pos_p)
        xp = xp + merge_branches(o_sb, o_df, g_sb, g_df, diff_subln_g[l], lam_init,
                                 w_sb_o[l], w_diff_o[l], w_out[l])
        mk, mv = memory_kv(mem_prompt, norm_mem[l], w_mk[l], w_mv[l])
        xp = xp + cross_attention(rms_norm(xp, norm_cross[l]), mk, mv, w_mq[l], w_mo[l])
        f, conv_p = conv_ffn(rms_norm(xp, norm_ffn[l]), zeros_conv, w_gate[l], w_up[l],
                             conv_w[l], conv_b[l], w_down[l])
        xp = xp + f
        p_sb_k.append(k_sb); p_sb_v.append(v_sb); p_df_k.append(k_df); p_df_v.append(v_df)
        p_mem_k.append(mk); p_mem_v.append(mv); p_conv.append(conv_p)

        h = rms_norm(xs, norm_mix[l])
        qs_sb, ks_sb, vs_sb, qs_df, ks_df, vs_df, gs_sb, gs_df = project_in(h, w_in[l])

        def attend_seq(args, l=l, lam=lam):
            pages, q1, k1, v1, q2, k2, v2 = args

            def gather(cache, new):
                past = cache[l, pages].reshape((past_len,) + cache.shape[3:])
                return jnp.concatenate([past.astype(new.dtype), new], axis=0)[None]

            o1 = stick_breaking(q1[None], gather(cache_sb_k, k1), gather(cache_sb_v, v1),
                                pos_sq, pos_sk)[0]
            o2 = diff_attention(q2[None], gather(cache_diff_k, k2), gather(cache_diff_v, v2),
                                pos_sq, pos_sk, t5_bias, lam)[0]
            return o1, o2

        os_sb, os_df = lax.map(attend_seq, (page_table, qs_sb, ks_sb, vs_sb, qs_df, ks_df, vs_df))
        xs = xs + merge_branches(os_sb, os_df, gs_sb, gs_df, diff_subln_g[l], lam_init,
                                 w_sb_o[l], w_diff_o[l], w_out[l])
        xs = xs + cross_attention(rms_norm(xs, norm_cross[l]), cache_mem_k[l], cache_mem_v[l],
                                  w_mq[l], w_mo[l])
        f, conv_s = conv_ffn(rms_norm(xs, norm_ffn[l]), state_conv[l], w_gate[l], w_up[l],
                             conv_w[l], conv_b[l], w_down[l])
        xs = xs + f
        s_sb_k.append(ks_sb); s_sb_v.append(vs_sb); s_df_k.append(ks_df); s_df_v.append(vs_df)
        s_conv.append(conv_s)

    y_prompt = rms_norm(xp, norm_f)
    y_sample = rms_norm(xs, norm_f)
    return (y_prompt, y_sample,
            jnp.stack(p_sb_k), jnp.stack(p_sb_v), jnp.stack(p_df_k), jnp.stack(p_df_v),
            jnp.stack(p_mem_k), jnp.stack(p_mem_v), jnp.stack(p_conv),
            jnp.stack(s_sb_k), jnp.stack(s_sb_v), jnp.stack(s_df_k), jnp.stack(s_df_v),
            jnp.stack(s_conv))
```

```python
import functools
import math

import jax
import jax.numpy as jnp
from jax import lax
from jax.experimental import pallas as pl
from jax.experimental.pallas import tpu as pltpu

F32 = jnp.float32
BF16 = jnp.bfloat16
EPS = 1e-6
NUM_BUCKETS = 32
MAX_DISTANCE = 128
CONV_W = 3
LANES = 128
SUBLANES = 8
VMEM_LIMIT = 56 * 1024 * 1024
NEG = -1e30
SB_EXIT = -120.0
SB_TK = 128
DF_BLK = 256
PAGES_PER_STEP = 8
QROWS = 16
HALO = 16


def _params(sem):
    return pltpu.CompilerParams(dimension_semantics=sem, vmem_limit_bytes=VMEM_LIMIT)


def _dot(a, b):
    return jnp.dot(a, b, preferred_element_type=F32)


def _dot_nt(a, b):
    return lax.dot_general(a, b, (((1,), (1,)), ((), ())), preferred_element_type=F32)


def _rms(x, g):
    return x * lax.rsqrt(jnp.mean(x * x, axis=-1, keepdims=True) + EPS) * g


def _rms_kernel(x_ref, g_ref, o_ref):
    o_ref[...] = _rms(x_ref[...], g_ref[...]).astype(o_ref.dtype)


def _rmsnorm_bf16(x, g):
    m, d = x.shape
    tm = min(m, 512)
    return pl.pallas_call(
        _rms_kernel,
        out_shape=jax.ShapeDtypeStruct((m, d), BF16),
        grid=(m // tm,),
        in_specs=[pl.BlockSpec((tm, d), lambda i: (i, 0)), pl.BlockSpec((1, d), lambda i: (0, 0))],
        out_specs=pl.BlockSpec((tm, d), lambda i: (i, 0)),
        compiler_params=_params(("parallel",)),
        name="rmsnorm",
    )(x, g.reshape(1, d))


def _linear_kernel(*refs, scale, sigmoid, has_res, has_norm, n_plain):
    x_ref, w_ref = refs[0], refs[1]
    pos = 2
    res_ref = g_ref = None
    if has_res:
        res_ref = refs[pos]
        pos += 1
    if has_norm:
        g_ref = refs[pos]
        pos += 1
    outs = refs[pos:]
    y = _dot(x_ref[...], w_ref[...])
    if scale is not None:
        y = y * scale
    if sigmoid:
        y = 1.0 / (1.0 + jnp.exp(-y))
    if has_res:
        y = y + res_ref[...]
    for o in outs[:n_plain]:
        o[...] = y.astype(o.dtype)
    if has_norm:
        outs[n_plain][...] = _rms(y, g_ref[...]).astype(BF16)


def _linear(x, w, col0, ncols, out_dtypes, *, scale=None, sigmoid=False, res=None, norm_g=None, tn=1024):
    m, k = x.shape
    tm = min(m, 512)
    tn = min(tn, ncols)
    if norm_g is not None:
        tn = ncols
    assert m % tm == 0 and ncols % tn == 0 and col0 % tn == 0
    off = col0 // tn
    in_specs = [pl.BlockSpec((tm, k), lambda j, i: (i, 0)),
                pl.BlockSpec((k, tn), lambda j, i: (0, j + off))]
    args = [x, w]
    if res is not None:
        in_specs.append(pl.BlockSpec((tm, tn), lambda j, i: (i, j)))
        args.append(res)
    if norm_g is not None:
        in_specs.append(pl.BlockSpec((1, tn), lambda j, i: (0, 0)))
        args.append(norm_g.reshape(1, ncols))
    dts = list(out_dtypes) + ([BF16] if norm_g is not None else [])
    out = pl.pallas_call(
        functools.partial(_linear_kernel, scale=scale, sigmoid=sigmoid, has_res=res is not None,
                          has_norm=norm_g is not None, n_plain=len(out_dtypes)),
        out_shape=[jax.ShapeDtypeStruct((m, ncols), dt) for dt in dts],
        grid=(ncols // tn, m // tm),
        in_specs=in_specs,
        out_specs=[pl.BlockSpec((tm, tn), lambda j, i: (i, j)) for _ in dts],
        compiler_params=_params(("parallel", "arbitrary")),
        name="linear",
    )(*args)
    return out


def _sb_consts(tk):
    r = lax.broadcasted_iota(jnp.int32, (tk, tk + LANES), 0)
    c = lax.broadcasted_iota(jnp.int32, (tk, tk + LANES), 1)
    return jnp.where((r > c) | (c >= tk), 1.0, 0.0).astype(BF16)


def _sb_chunk(q, ks, vs, u2, acc_ref, c_ref, vis):
    tk = ks.shape[0]
    z = _dot_nt(q, ks)
    lk = -(jnp.maximum(z, 0.0) + jnp.log(1.0 + jnp.exp(-jnp.abs(z))))
    if vis is not None:
        lk = jnp.where(vis, lk, 0.0)
    hi = lk.astype(BF16)
    lo = (lk - hi.astype(F32)).astype(BF16)
    cs = _dot(hi, u2) + _dot(lo, u2)
    later, total = cs[:, :tk], cs[:, tk:]
    c = c_ref[...]
    a = jnp.exp(z + lk + later + c)
    if vis is not None:
        a = jnp.where(vis, a, 0.0)
    acc_ref[...] += _dot(a.astype(BF16), vs)
    c_ref[...] = c + total


def _sb_prompt_kernel(q_ref, k_ref, v_ref, o_ref, acc_ref, c_ref, *, tq):
    tk = SB_TK
    i = pl.program_id(1)
    nd = tq // tk
    acc_ref[...] = jnp.zeros_like(acc_ref)
    c_ref[...] = jnp.zeros_like(c_ref)
    q = q_ref[...]
    u2 = _sb_consts(tk)
    row = lax.broadcasted_iota(jnp.int32, (tq, tk), 0)
    col = lax.broadcasted_iota(jnp.int32, (tq, tk), 1)
    for d in range(nd - 1, -1, -1):
        start = pl.multiple_of(i * tq + d * tk, tk)
        vis = (col + d * tk) < row
        _sb_chunk(q, k_ref[pl.ds(start, tk), :], v_ref[pl.ds(start, tk), :], u2, acc_ref, c_ref, vis)

    def live():
        return (jnp.max(c_ref[...]) > SB_EXIT).astype(jnp.int32)

    def cond(s):
        return jnp.logical_and(s[0] >= 0, s[1] > 0)

    def body(s):
        start = pl.multiple_of(s[0] * tk, tk)
        _sb_chunk(q, k_ref[pl.ds(start, tk), :], v_ref[pl.ds(start, tk), :], u2, acc_ref, c_ref, None)
        return s[0] - 1, live()

    lax.while_loop(cond, body, (i * nd - 1, live()))
    o_ref[...] = acc_ref[...].astype(o_ref.dtype)


def _sb_prompt(q, k, v, n_heads, dh):
    t = q.shape[0]
    tq = min(t, 256)
    return pl.pallas_call(
        functools.partial(_sb_prompt_kernel, tq=tq),
        out_shape=jax.ShapeDtypeStruct((t, n_heads * dh), BF16),
        grid=(n_heads, t // tq),
        in_specs=[pl.BlockSpec((tq, dh), lambda h, i: (i, h)),
                  pl.BlockSpec((t, dh), lambda h, i: (0, h)),
                  pl.BlockSpec((t, dh), lambda h, i: (0, h))],
        out_specs=pl.BlockSpec((tq, dh), lambda h, i: (i, h)),
        scratch_shapes=[pltpu.VMEM((tq, dh), F32), pltpu.VMEM((tq, LANES), F32)],
        compiler_params=_params(("parallel", "arbitrary")),
        name="sb_prompt",
    )(q, k, v)


def _softmax_step(z, vs, m_ref, l_ref, acc_ref):
    m_old = m_ref[...]
    m_new = jnp.maximum(m_old, jnp.max(z, axis=1, keepdims=True))
    alpha = jnp.exp(m_old - m_new)
    p = jnp.exp(z - m_new)
    l_ref[...] = alpha * l_ref[...] + jnp.sum(p, axis=1, keepdims=True)
    acc_ref[...] = alpha * acc_ref[...] + _dot(p.astype(BF16), vs)
    m_ref[...] = m_new


def _df_prompt_kernel(far_ref, lam_ref, q_ref, k_ref, v_ref, tiles_ref, g_ref, o_ref,
                      m_ref, l_ref, acc_ref, *, dh, out_scale):
    blk = DF_BLK
    h = pl.program_id(0)
    i = pl.program_id(1)
    m_ref[...] = jnp.full_like(m_ref, NEG)
    l_ref[...] = jnp.zeros_like(l_ref)
    acc_ref[...] = jnp.zeros_like(acc_ref)
    q = q_ref[...]

    def step(kb, bias):
        start = pl.multiple_of(kb * blk, blk)
        ks = k_ref[pl.ds(start, blk), :]
        vs = v_ref[pl.ds(start, blk), :]
        for c in range(2):
            z = _dot_nt(q[:, c * dh:(c + 1) * dh], ks[:, c * dh:(c + 1) * dh]) + bias
            _softmax_step(z, vs, m_ref.at[c], l_ref.at[c], acc_ref.at[c])

    far = far_ref[h]

    def far_body(kb, carry):
        step(kb, far)
        return carry

    lax.fori_loop(0, jnp.maximum(i - 1, 0), far_body, 0)

    @pl.when(i >= 1)
    def _():
        step(i - 1, tiles_ref[1])

    step(i, tiles_ref[0])
    o = acc_ref[0] / l_ref[0] - lam_ref[0] * (acc_ref[1] / l_ref[1])
    o_ref[...] = (_rms(o, g_ref[...]) * out_scale).astype(o_ref.dtype)


def _df_prompt(q, k, v, tiles, far, lam, g, n_heads, dh, out_scale):
    t = q.shape[0]
    blk = DF_BLK
    dv = 2 * dh
    assert t % blk == 0
    grid_spec = pltpu.PrefetchScalarGridSpec(
        num_scalar_prefetch=2,
        grid=(n_heads, t // blk),
        in_specs=[pl.BlockSpec((blk, dv), lambda h, i, *_: (i, h)),
                  pl.BlockSpec((t, dv), lambda h, i, *_: (0, h)),
                  pl.BlockSpec((t, dv), lambda h, i, *_: (0, h)),
                  pl.BlockSpec((None, 2, blk, blk), lambda h, i, *_: (h, 0, 0, 0)),
                  pl.BlockSpec((1, dv), lambda h, i, *_: (0, 0))],
        out_specs=pl.BlockSpec((blk, dv), lambda h, i, *_: (i, h)),
        scratch_shapes=[pltpu.VMEM((2, blk, 1), F32), pltpu.VMEM((2, blk, 1), F32),
                        pltpu.VMEM((2, blk, dv), F32)],
    )
    return pl.pallas_call(
        functools.partial(_df_prompt_kernel, dh=dh, out_scale=out_scale),
        out_shape=jax.ShapeDtypeStruct((t, n_heads * dv), BF16),
        grid_spec=grid_spec,
        compiler_params=_params(("parallel", "arbitrary")),
        name="df_prompt",
    )(far, lam, q, k, v, tiles, g.reshape(1, dv))


def _df_decode_kernel(pt_ref, far_ref, lam_ref, q_ref, blast_ref, bnew_ref, g_ref, knew_ref, vnew_ref, *rest,
                      n_heads, dh, ts, out_scale):
    pg = PAGES_PER_STEP
    k_refs, v_refs = rest[:pg], rest[pg:2 * pg]
    o_ref, kbf, vbf, m_ref, l_ref, acc_ref = rest[2 * pg:]
    dv = 2 * dh
    s = pl.program_id(1)
    last = s == pl.num_programs(1) - 1
    page = k_refs[0].shape[0]

    @pl.when(s == 0)
    def _():
        m_ref[...] = jnp.full_like(m_ref, NEG)
        l_ref[...] = jnp.zeros_like(l_ref)
        acc_ref[...] = jnp.zeros_like(acc_ref)

    for r in range(pg):
        kbf[r * page:(r + 1) * page, :] = k_refs[r][...].astype(BF16)
        vbf[r * page:(r + 1) * page, :] = v_refs[r][...].astype(BF16)
    for h in range(n_heads):
        z = _dot_nt(q_ref[h], kbf[:, h * dv:(h + 1) * dv])
        z = z + jnp.where(last, blast_ref[h], far_ref[h])
        _softmax_step(z, vbf[:, h * dv:(h + 1) * dv], m_ref.at[h], l_ref.at[h], acc_ref.at[h])

    @pl.when(last)
    def _():
        pad = jnp.zeros((page - knew_ref.shape[0], dv), F32)
        for h in range(n_heads):
            kn = jnp.concatenate([knew_ref[:, h * dv:(h + 1) * dv], pad], axis=0).astype(BF16)
            vn = jnp.concatenate([vnew_ref[:, h * dv:(h + 1) * dv], pad], axis=0).astype(BF16)
            z = _dot_nt(q_ref[h], kn) + bnew_ref[h]
            _softmax_step(z, vn, m_ref.at[h], l_ref.at[h], acc_ref.at[h])
            on = acc_ref[h] / l_ref[h]
            o = on - lam_ref[0] * pltpu.roll(on, on.shape[0] - ts, 0)
            o_ref[:, h * dv:(h + 1) * dv] = _rms(o, g_ref[...]) * out_scale


def _df_decode(page_table, far, lam, qbd, blast, bnew, g, knew, vnew, kcache, vcache, n_heads, dh, ts, out_scale):
    nb, n_pages = page_table.shape
    pg = PAGES_PER_STEP
    assert n_pages % pg == 0
    page, width = kcache.shape[1], kcache.shape[2]
    dv = 2 * dh
    rows = qbd.shape[2]

    def page_spec(r):
        return pl.BlockSpec((None, page, width), lambda b, s, pt, *_: (pt[b * n_pages + s * pg + r], 0, 0))

    grid_spec = pltpu.PrefetchScalarGridSpec(
        num_scalar_prefetch=3,
        grid=(nb, n_pages // pg),
        in_specs=[pl.BlockSpec((None, n_heads, rows, dv), lambda b, s, *_: (b, 0, 0, 0)),
                  pl.BlockSpec((n_heads, rows, pg * page), lambda b, s, *_: (0, 0, 0)),
                  pl.BlockSpec((n_heads, rows, page), lambda b, s, *_: (0, 0, 0)),
                  pl.BlockSpec((1, dv), lambda b, s, *_: (0, 0)),
                  pl.BlockSpec((None, SUBLANES, width), lambda b, s, *_: (b, 0, 0)),
                  pl.BlockSpec((None, SUBLANES, width), lambda b, s, *_: (b, 0, 0))]
                 + [page_spec(r) for r in range(pg)] + [page_spec(r) for r in range(pg)],
        out_specs=pl.BlockSpec((None, rows, width), lambda b, s, *_: (b, 0, 0)),
        scratch_shapes=[pltpu.VMEM((pg * page, width), BF16), pltpu.VMEM((pg * page, width), BF16),
                        pltpu.VMEM((n_heads, rows, 1), F32), pltpu.VMEM((n_heads, rows, 1), F32),
                        pltpu.VMEM((n_heads, rows, dv), F32)],
    )
    return pl.pallas_call(
        functools.partial(_df_decode_kernel, n_heads=n_heads, dh=dh, ts=ts, out_scale=out_scale),
        out_shape=jax.ShapeDtypeStruct((nb, rows, width), F32),
        grid_spec=grid_spec,
        compiler_params=_params(("parallel", "arbitrary")),
        name="df_decode",
    )(page_table.reshape(-1), far, lam, qbd, blast, bnew, g.reshape(1, dv), knew, vnew,
      *([kcache] * pg), *([vcache] * pg))


def _sb_decode_kernel(pt_ref, q_ref, knew_ref, vnew_ref, *rest, n_heads, dh, ts):
    pg = PAGES_PER_STEP
    k_refs, v_refs = rest[:pg], rest[pg:2 * pg]
    o_ref, kbf, vbf, acc_ref, c_ref, live_ref = rest[2 * pg:]
    s = pl.program_id(1)
    page = k_refs[0].shape[0]
    rows = q_ref.shape[1]
    u2 = _sb_consts(page)

    @pl.when(s == 0)
    def _():
        acc_ref[...] = jnp.zeros_like(acc_ref)
        c_ref[...] = jnp.zeros_like(c_ref)
        row = lax.broadcasted_iota(jnp.int32, (rows, page), 0)
        col = lax.broadcasted_iota(jnp.int32, (rows, page), 1)
        vis = col < jnp.minimum(row, ts)
        pad = jnp.zeros((page - knew_ref.shape[0], dh), F32)
        for h in range(n_heads):
            kn = jnp.concatenate([knew_ref[:, h * dh:(h + 1) * dh], pad], axis=0).astype(BF16)
            vn = jnp.concatenate([vnew_ref[:, h * dh:(h + 1) * dh], pad], axis=0).astype(BF16)
            _sb_chunk(q_ref[h], kn, vn, u2, acc_ref.at[h], c_ref.at[h], vis)
        live_ref[0] = (jnp.max(c_ref[...]) > SB_EXIT).astype(jnp.int32)

    @pl.when(live_ref[0] > 0)
    def _():
        for r in range(pg):
            kbf[r * page:(r + 1) * page, :] = k_refs[r][...].astype(BF16)
            vbf[r * page:(r + 1) * page, :] = v_refs[r][...].astype(BF16)
        for r in range(pg - 1, -1, -1):
            for h in range(n_heads):
                _sb_chunk(q_ref[h], kbf[r * page:(r + 1) * page, h * dh:(h + 1) * dh],
                          vbf[r * page:(r + 1) * page, h * dh:(h + 1) * dh], u2,
                          acc_ref.at[h], c_ref.at[h], None)
        live_ref[0] = (jnp.max(c_ref[...]) > SB_EXIT).astype(jnp.int32)

    @pl.when(s == pl.num_programs(1) - 1)
    def _():
        for h in range(n_heads):
            o_ref[:, h * dh:(h + 1) * dh] = acc_ref[h]


def _sb_decode(page_table, q, knew, vnew, kcache, vcache, n_heads, dh, ts):
    nb, n_pages = page_table.shape
    pg = PAGES_PER_STEP
    assert n_pages % pg == 0
    n_steps = n_pages // pg
    page, width = kcache.shape[1], kcache.shape[2]
    rows = q.shape[2]

    def page_spec(r):
        return pl.BlockSpec((None, page, width),
                            lambda b, s, pt: (pt[b * n_pages + (n_steps - 1 - s) * pg + r], 0, 0))

    grid_spec = pltpu.PrefetchScalarGridSpec(
        num_scalar_prefetch=1,
        grid=(nb, n_steps),
        in_specs=[pl.BlockSpec((None, n_heads, rows, dh), lambda b, s, pt: (b, 0, 0, 0)),
                  pl.BlockSpec((None, SUBLANES, width), lambda b, s, pt: (b, 0, 0)),
                  pl.BlockSpec((None, SUBLANES, width), lambda b, s, pt: (b, 0, 0))]
                 + [page_spec(r) for r in range(pg)] + [page_spec(r) for r in range(pg)],
        out_specs=pl.BlockSpec((None, rows, width), lambda b, s, pt: (b, 0, 0)),
        scratch_shapes=[pltpu.VMEM((pg * page, width), BF16), pltpu.VMEM((pg * page, width), BF16),
                        pltpu.VMEM((n_heads, rows, dh), F32), pltpu.VMEM((n_heads, rows, LANES), F32),
                        pltpu.SMEM((1,), jnp.int32)],
    )
    return pl.pallas_call(
        functools.partial(_sb_decode_kernel, n_heads=n_heads, dh=dh, ts=ts),
        out_shape=jax.ShapeDtypeStruct((nb, rows, width), F32),
        grid_spec=grid_spec,
        compiler_params=_params(("parallel", "arbitrary")),
        name="sb_decode",
    )(page_table.reshape(-1), q, knew, vnew, *([kcache] * pg), *([vcache] * pg))


def _merge_kernel(osb_ref, odf_ref, gsb_ref, gdf_ref, x_ref, wsb_ref, wdf_ref, wout_ref, g_ref, x1_ref, h_ref):
    y = (gsb_ref[...].astype(F32) * _dot(osb_ref[...], wsb_ref[...])
         + gdf_ref[...].astype(F32) * _dot(odf_ref[...], wdf_ref[...]))
    x1 = x_ref[...] + _dot(y.astype(BF16), wout_ref[...])
    x1_ref[...] = x1
    h_ref[...] = _rms(x1, g_ref[...]).astype(BF16)


def _merge(o_sb, o_df, gates, x, w_sb_o, w_diff_o, w_out, norm_g):
    m, d = x.shape
    w1 = o_sb.shape[1]
    tm = min(m, 256)
    const = lambda i: (0, 0)
    return pl.pallas_call(
        _merge_kernel,
        out_shape=[jax.ShapeDtypeStruct((m, d), F32), jax.ShapeDtypeStruct((m, d), BF16)],
        grid=(m // tm,),
        in_specs=[pl.BlockSpec((tm, w1), lambda i: (i, 0)),
                  pl.BlockSpec((tm, w1), lambda i: (i, 0)),
                  pl.BlockSpec((tm, d), lambda i: (i, 0)),
                  pl.BlockSpec((tm, d), lambda i: (i, 1)),
                  pl.BlockSpec((tm, d), lambda i: (i, 0)),
                  pl.BlockSpec((w1, d), const, pipeline_mode=pl.Buffered(1)),
                  pl.BlockSpec((w1, d), const, pipeline_mode=pl.Buffered(1)),
                  pl.BlockSpec((d, d), const, pipeline_mode=pl.Buffered(1)),
                  pl.BlockSpec((1, d), const)],
        out_specs=[pl.BlockSpec((tm, d), lambda i: (i, 0)), pl.BlockSpec((tm, d), lambda i: (i, 0))],
        compiler_params=_params(("parallel",)),
        name="merge",
    )(o_sb, o_df, gates, gates, x, w_sb_o, w_diff_o, w_out, norm_g.reshape(1, d))


def _mem_attn_kernel(q_ref, mk_ref, mv_ref, o_ref, *, n_heads, dh):
    for h in range(n_heads):
        sl = slice(h * dh, (h + 1) * dh)
        z = _dot_nt(q_ref[:, sl], mk_ref[:, sl].astype(BF16))
        p = jnp.exp(z - jnp.max(z, axis=1, keepdims=True))
        o = _dot(p.astype(BF16), mv_ref[:, sl].astype(BF16)) / jnp.sum(p, axis=1, keepdims=True)
        o_ref[:, sl] = o.astype(o_ref.dtype)


def _mem_attn(q, mk, mv, n_heads, dh):
    nb, m, w = q.shape
    n_mem = mk.shape[1]
    tm = min(m, 512)
    return pl.pallas_call(
        functools.partial(_mem_attn_kernel, n_heads=n_heads, dh=dh),
        out_shape=jax.ShapeDtypeStruct((nb, m, w), BF16),
        grid=(nb, m // tm),
        in_specs=[pl.BlockSpec((None, tm, w), lambda b, i: (b, i, 0)),
                  pl.BlockSpec((None, n_mem, w), lambda b, i: (b, 0, 0)),
                  pl.BlockSpec((None, n_mem, w), lambda b, i: (b, 0, 0))],
        out_specs=pl.BlockSpec((None, tm, w), lambda b, i: (b, i, 0)),
        compiler_params=_params(("parallel", "arbitrary")),
        name="mem_attn",
    )(q, mk, mv)


def _ffn_kernel(*refs, sample, ts, tail):
    if sample:
        h_ref, p1_ref, p2_ref = refs[:3]
        pos = 3
    else:
        h_ref, halo_ref = refs[:2]
        pos = 2
    (x_ref, wg_ref, wu_ref, wd_ref, cw_ref, cb_ref, gf_ref, y_ref, gout_ref, acc_ref) = refs[pos:]
    i = pl.program_id(0)
    f = pl.program_id(1)
    tm, tf = h_ref.shape[0], wg_ref.shape[1]

    @pl.when(f == 0)
    def _():
        acc_ref[...] = jnp.zeros_like(acc_ref)

    h = h_ref[...]
    g = _dot(h, wg_ref[...])
    u = _dot(h, wu_ref[...])
    g1 = pltpu.roll(g, 1, 0)
    g2 = pltpu.roll(g, 2, 0)
    if sample:
        gout_ref[...] = g
        t = lax.rem(lax.broadcasted_iota(jnp.int32, (tm, tf), 0), ts)
        g1 = jnp.where(t >= 1, g1, p1_ref[...])
        g2 = jnp.where(t >= 2, g2, p2_ref[...])
    else:
        gout_ref[...] = g[tm - tail:, :]
        gh = _dot(halo_ref[...], wg_ref[...]) * (i > 0).astype(F32)
        row = lax.broadcasted_iota(jnp.int32, (SUBLANES, tf), 0)
        top1 = jnp.where(row < 1, pltpu.roll(gh, 1, 0)[:SUBLANES], g1[:SUBLANES])
        top2 = jnp.where(row < 2, pltpu.roll(gh, 2, 0)[:SUBLANES], g2[:SUBLANES])
        g1 = jnp.concatenate([top1, g1[SUBLANES:]], axis=0)
        g2 = jnp.concatenate([top2, g2[SUBLANES:]], axis=0)
    cw = cw_ref[...]
    c = cb_ref[...] + cw[0:1] * g2 + cw[1:2] * g1 + cw[2:3] * g
    a = c / (1.0 + jnp.exp(-c)) * u
    acc_ref[...] += _dot(a.astype(BF16), wd_ref[...])

    @pl.when(f == pl.num_programs(1) - 1)
    def _():
        y_ref[...] = _rms(x_ref[...] + acc_ref[...], gf_ref[...])


def _ffn(h, x, wg, wu, wd, cw, cb, norm_f, *, prev=None, ts=1, tf=512):
    m, d = x.shape
    fp = wg.shape[1]
    sample = prev is not None
    tm = min(m, 512)
    tail = SUBLANES
    assert fp % tf == 0 and m % tm == 0
    row = lambda i, f: (i, 0)
    in_specs = [pl.BlockSpec((tm, d), row)]
    args = [h]
    if sample:
        in_specs += [pl.BlockSpec((tm, tf), lambda i, f: (i, f))] * 2
        args += list(prev)
        g_shape, g_spec = (m, fp), pl.BlockSpec((tm, tf), lambda i, f: (i, f))
    else:
        halo_blocks = tm // HALO
        in_specs.append(pl.BlockSpec((HALO, d), lambda i, f: (jnp.maximum(i * halo_blocks - 1, 0), 0)))
        args.append(h)
        g_shape, g_spec = (m // tm * tail, fp), pl.BlockSpec((tail, tf), lambda i, f: (i, f))
    in_specs += [pl.BlockSpec((tm, d), row),
                 pl.BlockSpec((d, tf), lambda i, f: (0, f)),
                 pl.BlockSpec((d, tf), lambda i, f: (0, f)),
                 pl.BlockSpec((tf, d), lambda i, f: (f, 0)),
                 pl.BlockSpec((CONV_W, tf), lambda i, f: (0, f)),
                 pl.BlockSpec((1, tf), lambda i, f: (0, f)),
                 pl.BlockSpec((1, d), lambda i, f: (0, 0))]
    args += [x, wg, wu, wd, cw, cb.reshape(1, fp), norm_f.reshape(1, d)]
    return pl.pallas_call(
        functools.partial(_ffn_kernel, sample=sample, ts=ts, tail=tail),
        out_shape=[jax.ShapeDtypeStruct((m, d), F32), jax.ShapeDtypeStruct(g_shape, F32)],
        grid=(m // tm, fp // tf),
        in_specs=in_specs,
        out_specs=[pl.BlockSpec((tm, d), row), g_spec],
        scratch_shapes=[pltpu.VMEM((tm, d), F32)],
        compiler_params=_params(("arbitrary", "arbitrary")),
        name="conv_ffn",
    )(*args)


def _t5_bias_by_distance(t5_bias, n):
    d = jnp.arange(n, dtype=jnp.int32)
    max_exact = NUM_BUCKETS // 2
    df = jnp.maximum(d, 1).astype(F32)
    large = max_exact + (jnp.log(df / max_exact) / math.log(MAX_DISTANCE / max_exact)
                         * (NUM_BUCKETS - max_exact)).astype(jnp.int32)
    large = jnp.minimum(large, NUM_BUCKETS - 1)
    bucket = jnp.where(d < max_exact, d, large)
    return t5_bias.astype(F32)[bucket].T


def _toeplitz(bias_d, rel):
    out = bias_d[:, jnp.clip(rel, 0, bias_d.shape[1] - 1)]
    return jnp.where(rel >= 0, out, NEG)


def _pad_rows(a, rows):
    pad = [(0, 0)] * a.ndim
    pad[-2] = (0, rows - a.shape[-2])
    return jnp.pad(a, pad)


def kernel(x_prompt, x_sample, mem_prompt, cache_sb_k, cache_sb_v, cache_diff_k, cache_diff_v, cache_mem_k, cache_mem_v, state_conv, page_table, norm_mix, w_in, w_sb_o, w_diff_o, w_out, diff_subln_g, lambda_q1, lambda_k1, lambda_q2, lambda_k2, t5_bias, norm_cross, norm_mem, w_mq, w_mk, w_mv, w_mo, norm_ffn, w_gate, w_up, conv_w, conv_b, w_down, norm_f):
    depth = w_in.shape[0]
    assert depth == 1 and x_prompt.shape[0] == 1
    _, t, d = x_prompt.shape
    nb, ts, _ = x_sample.shape
    n_pool, page, h_sb, dh = cache_sb_k.shape[1:]
    h_df = cache_diff_k.shape[3]
    dv = 2 * dh
    h_mem = cache_mem_k.shape[3]
    n_mem = mem_prompt.shape[1]
    d_ff = w_gate.shape[2]
    n_pages = page_table.shape[1]
    past = n_pages * page
    sb_w = h_sb * dh
    scale = dh ** -0.5
    lam_init = 0.8 - 0.6 * math.exp(-0.3 * 0)
    out_scale = 1.0 - lam_init
    assert 2 * ts <= QROWS and ts >= CONV_W - 1 and page == LANES and DF_BLK >= MAX_DISTANCE

    lam = (jnp.exp(jnp.sum(lambda_q1[0].astype(F32) * lambda_k1[0].astype(F32)))
           - jnp.exp(jnp.sum(lambda_q2[0].astype(F32) * lambda_k2[0].astype(F32))) + lam_init).reshape(1)

    bias_d = _t5_bias_by_distance(t5_bias, 2 * DF_BLK)
    far = bias_d[:, MAX_DISTANCE]
    ar = jnp.arange(DF_BLK, dtype=jnp.int32)
    rel0 = ar[:, None] - ar[None, :]
    tiles = jnp.stack([_toeplitz(bias_d, rel0), _toeplitz(bias_d, rel0 + DF_BLK)], axis=1)

    tf = 512
    fp = ((d_ff + tf - 1) // tf) * tf
    w_in_b = w_in[0].astype(BF16)
    w_sb_o_b, w_diff_o_b, w_out_b = w_sb_o[0].astype(BF16), w_diff_o[0].astype(BF16), w_out[0].astype(BF16)
    w_mq_b, w_mk_b, w_mv_b, w_mo_b = (w[0].astype(BF16) for w in (w_mq, w_mk, w_mv, w_mo))
    wg_b = jnp.pad(w_gate[0].astype(BF16), ((0, 0), (0, fp - d_ff)))
    wu_b = jnp.pad(w_up[0].astype(BF16), ((0, 0), (0, fp - d_ff)))
    wd_b = jnp.pad(w_down[0].astype(BF16), ((0, fp - d_ff), (0, 0)))
    cw_p = jnp.pad(conv_w[0], ((0, 0), (0, fp - d_ff)))
    cb_p = jnp.pad(conv_b[0], ((0, fp - d_ff),))

    def project(x2d):
        h = _rmsnorm_bf16(x2d, norm_mix[0])
        c = 0
        q_sb, = _linear(h, w_in_b, c, sb_w, [BF16], scale=scale); c += sb_w
        k_sb, k_sb_b = _linear(h, w_in_b, c, sb_w, [F32, BF16]); c += sb_w
        v_sb, v_sb_b = _linear(h, w_in_b, c, sb_w, [F32, BF16]); c += sb_w
        q_df, = _linear(h, w_in_b, c, h_df * dv, [BF16], scale=scale); c += h_df * dv
        k_df, k_df_b = _linear(h, w_in_b, c, h_df * dv, [F32, BF16]); c += h_df * dv
        v_df, v_df_b = _linear(h, w_in_b, c, h_df * dv, [F32, BF16]); c += h_df * dv
        gates, = _linear(h, w_in_b, c, 2 * d, [BF16], sigmoid=True, tn=2048)
        return q_sb, (k_sb, k_sb_b), (v_sb, v_sb_b), q_df, (k_df, k_df_b), (v_df, v_df_b), gates

    def tail(x2d, o_sb, o_df, gates, mk, mv, rows_per_mem, ffn_prev, ffn_ts):
        x1, h2 = _merge(o_sb, o_df, gates, x2d, w_sb_o_b, w_diff_o_b, w_out_b, norm_cross[0])
        qm, = _linear(h2, w_mq_b, 0, h_mem * dh, [BF16], scale=scale)
        nbm = mk.shape[0]
        qm = qm.reshape(nbm, rows_per_mem, h_mem * dh)
        if rows_per_mem < QROWS:
            om = _mem_attn(_pad_rows(qm, QROWS), mk, mv, h_mem, dh)[:, :rows_per_mem]
        else:
            om = _mem_attn(qm, mk, mv, h_mem, dh)
        om = om.reshape(nbm * rows_per_mem, h_mem * dh)
        x2, h3 = _linear(om, w_mo_b, 0, d, [F32], res=x1, norm_g=norm_ffn[0])
        return _ffn(h3, x2, wg_b, wu_b, wd_b, cw_p, cb_p, norm_f, prev=ffn_prev, ts=ffn_ts, tf=tf)

    xp = x_prompt.reshape(t, d)
    q_sb, (k_sb, k_sb_b), (v_sb, v_sb_b), q_df, (k_df, k_df_b), (v_df, v_df_b), gates = project(xp)
    o_sb = _sb_prompt(q_sb, k_sb_b, v_sb_b, h_sb, dh)
    o_df = _df_prompt(q_df, k_df_b, v_df_b, tiles, far, lam, diff_subln_g[0], h_df, dh, out_scale)
    m_b = _rmsnorm_bf16(mem_prompt.reshape(n_mem, d), norm_mem[0])
    mk, = _linear(m_b, w_mk_b, 0, h_mem * dh, [F32])
    mv, = _linear(m_b, w_mv_b, 0, h_mem * dh, [F32])
    y_p, g_tail = tail(xp, o_sb, o_df, gates, mk[None], mv[None], t, None, 1)
    conv_p = g_tail[g_tail.shape[0] - (CONV_W - 1):, :d_ff]

    xs = x_sample.reshape(nb * ts, d)
    qs_sb, (ks_sb, _), (vs_sb, _), qs_df, (ks_df, _), (vs_df, _), gates_s = project(xs)
    q1 = _pad_rows(qs_sb.reshape(nb, ts, h_sb, dh).transpose(0, 2, 1, 3), QROWS)
    os_sb = _sb_decode(page_table, q1, _pad_rows(ks_sb.reshape(nb, ts, sb_w), SUBLANES),
                       _pad_rows(vs_sb.reshape(nb, ts, sb_w), SUBLANES),
                       cache_sb_k[0].reshape(n_pool, page, sb_w), cache_sb_v[0].reshape(n_pool, page, sb_w),
                       h_sb, dh, ts)[:, :ts].astype(BF16)
    q2 = qs_df.reshape(nb, ts, h_df, 2, dh).transpose(0, 2, 3, 1, 4)
    zq = jnp.zeros_like(q2[:, :, 0])
    qbd = jnp.concatenate([jnp.concatenate([q2[:, :, 0], zq], axis=-1),
                           jnp.concatenate([zq, q2[:, :, 1]], axis=-1)], axis=2)
    qbd = _pad_rows(qbd, QROWS)
    tok = jnp.pad(jnp.tile(jnp.arange(ts, dtype=jnp.int32), 2), (0, QROWS - 2 * ts))
    key = jnp.arange(PAGES_PER_STEP * page, dtype=jnp.int32)
    blast = _toeplitz(bias_d, PAGES_PER_STEP * page + tok[:, None] - key[None, :])
    keyn = jnp.arange(page, dtype=jnp.int32)
    bnew = _toeplitz(bias_d, jnp.where(keyn[None, :] < ts, tok[:, None] - keyn[None, :], -1))
    os_df = _df_decode(page_table, far, lam, qbd, blast, bnew, diff_subln_g[0],
                       _pad_rows(ks_df.reshape(nb, ts, h_df * dv), SUBLANES),
                       _pad_rows(vs_df.reshape(nb, ts, h_df * dv), SUBLANES),
                       cache_diff_k[0].reshape(n_pool, page, h_df * dv),
                       cache_diff_v[0].reshape(n_pool, page, h_df * dv), h_df, dh, ts,
                       out_scale)[:, :ts].astype(BF16)
    st = jnp.pad(state_conv[0], ((0, 0), (0, 0), (0, fp - d_ff)))
    zero = jnp.zeros((nb, 1, fp), F32)
    p1 = jnp.concatenate([st[:, 1:2]] + [zero] * (ts - 1), axis=1).reshape(nb * ts, fp)
    p2 = jnp.concatenate([st[:, 0:1], st[:, 1:2]] + [zero] * (ts - 2), axis=1).reshape(nb * ts, fp)
    y_s, g_s = tail(xs, os_sb.reshape(nb * ts, sb_w), os_df.reshape(nb * ts, h_df * dv), gates_s,
                    cache_mem_k[0].reshape(nb, n_mem, h_mem * dh), cache_mem_v[0].reshape(nb, n_mem, h_mem * dh),
                    ts, (p1, p2), ts)
    gp = jnp.concatenate([state_conv[0], g_s[:, :d_ff].reshape(nb, ts, d_ff)], axis=1)
    conv_s = gp[:, ts:]

    return (y_p.reshape(1, t, d), y_s.reshape(nb, ts, d),
            k_sb.reshape(1, 1, t, h_sb, dh), v_sb.reshape(1, 1, t, h_sb, dh),
            k_df.reshape(1, 1, t, h_df, 2, dh), v_df.reshape(1, 1, t, h_df, dv),
            mk.reshape(1, 1, n_mem, h_mem, dh), mv.reshape(1, 1, n_mem, h_mem, dh),
            conv_p.reshape(1, 1, CONV_W - 1, d_ff),
            ks_sb.reshape(1, nb, ts, h_sb, dh), vs_sb.reshape(1, nb, ts, h_sb, dh),
            ks_df.reshape(1, nb, ts, h_df, 2, dh), vs_df.reshape(1, nb, ts, h_df, dv),
            conv_s.reshape(1, nb, CONV_W - 1, d_ff))
```

```python
import functools
import math

import jax
import jax.numpy as jnp
from jax import lax
from jax.experimental import pallas as pl
from jax.experimental.pallas import tpu as pltpu

F32 = jnp.float32
BF16 = jnp.bfloat16
EPS = 1e-6
NUM_BUCKETS = 32
MAX_DISTANCE = 128
CONV_W = 3
LANES = 128
SUBLANES = 8
VMEM_LIMIT = 56 * 1024 * 1024
NEG = -1e30
LOG2E = 1.4426950408889634
SB_EXIT = -120.0
SB_TK = 128
SB_HEADS_PER_STEP = 4
DF_BLK = 256
DF_FAR_BLOCKS = 4
PAGES_PER_STEP = 8
QROWS = 16
HALO = 16


def _params(sem):
    return pltpu.CompilerParams(dimension_semantics=sem, vmem_limit_bytes=VMEM_LIMIT)


def _dot(a, b):
    return jnp.dot(a, b, preferred_element_type=F32)


def _dot_nt(a, b):
    return lax.dot_general(a, b, (((1,), (1,)), ((), ())), preferred_element_type=F32)


def _rms(x, g):
    return x * lax.rsqrt(jnp.mean(x * x, axis=-1, keepdims=True) + EPS) * g


def _rms_kernel(x_ref, g_ref, o_ref):
    o_ref[...] = _rms(x_ref[...], g_ref[...]).astype(o_ref.dtype)


def _rmsnorm_bf16(x, g):
    m, d = x.shape
    tm = min(m, 512)
    return pl.pallas_call(
        _rms_kernel,
        out_shape=jax.ShapeDtypeStruct((m, d), BF16),
        grid=(m // tm,),
        in_specs=[pl.BlockSpec((tm, d), lambda i: (i, 0)), pl.BlockSpec((1, d), lambda i: (0, 0))],
        out_specs=pl.BlockSpec((tm, d), lambda i: (i, 0)),
        compiler_params=_params(("parallel",)),
        name="rmsnorm",
    )(x, g.reshape(1, d))


def _linear_kernel(*refs, scale, sigmoid, has_res, has_norm, n_plain, transposed):
    x_ref, w_ref = refs[0], refs[1]
    pos = 2
    res_ref = g_ref = None
    if has_res:
        res_ref = refs[pos]
        pos += 1
    if has_norm:
        g_ref = refs[pos]
        pos += 1
    outs = refs[pos:]
    y = _dot(x_ref[...], w_ref[...])
    if scale is not None:
        y = y * scale
    if sigmoid:
        y = 1.0 / (1.0 + jnp.exp(-y))
    if has_res:
        y = y + res_ref[...]
    for o in outs[:n_plain]:
        o[...] = y.astype(o.dtype)
    if has_norm:
        outs[n_plain][...] = _rms(y, g_ref[...]).astype(BF16)
    if transposed:
        outs[-1][...] = y.T.astype(BF16)


def _linear(x, w, col0, ncols, out_dtypes, *, scale=None, sigmoid=False, res=None, norm_g=None, tn=1024,
            transposed=False):
    m, k = x.shape
    tm = min(m, 512)
    tn = min(tn, ncols)
    if norm_g is not None:
        tn = ncols
    assert m % tm == 0 and ncols % tn == 0 and col0 % tn == 0
    off = col0 // tn
    in_specs = [pl.BlockSpec((tm, k), lambda j, i: (i, 0)),
                pl.BlockSpec((k, tn), lambda j, i: (0, j + off))]
    args = [x, w]
    if res is not None:
        in_specs.append(pl.BlockSpec((tm, tn), lambda j, i: (i, j)))
        args.append(res)
    if norm_g is not None:
        in_specs.append(pl.BlockSpec((1, tn), lambda j, i: (0, 0)))
        args.append(norm_g.reshape(1, ncols))
    dts = list(out_dtypes) + ([BF16] if norm_g is not None else [])
    out_shape = [jax.ShapeDtypeStruct((m, ncols), dt) for dt in dts]
    out_specs = [pl.BlockSpec((tm, tn), lambda j, i: (i, j)) for _ in dts]
    if transposed:
        out_shape.append(jax.ShapeDtypeStruct((ncols, m), BF16))
        out_specs.append(pl.BlockSpec((tn, tm), lambda j, i: (j, i)))
    out = pl.pallas_call(
        functools.partial(_linear_kernel, scale=scale, sigmoid=sigmoid, has_res=res is not None,
                          has_norm=norm_g is not None, n_plain=len(out_dtypes), transposed=transposed),
        out_shape=out_shape,
        grid=(ncols // tn, m // tm),
        in_specs=in_specs,
        out_specs=out_specs,
        compiler_params=_params(("parallel", "arbitrary")),
        name="linear",
    )(*args)
    return out


def _sb_consts(tk):
    r = lax.broadcasted_iota(jnp.int32, (tk, tk + LANES), 0)
    c = lax.broadcasted_iota(jnp.int32, (tk, tk + LANES), 1)
    return jnp.where((r > c) | (c >= tk), 1.0, 0.0).astype(BF16)


def _sb_chunk(qs, ks, vs, u2, acc_refs, c_refs, vis):
    tk = ks[0].shape[0]
    zs = [_dot_nt(q, k) for q, k in zip(qs, ks)]
    lks = []
    for z in zs:
        lk = -(jnp.maximum(z, 0.0) + jnp.log(1.0 + jnp.exp(-jnp.abs(z))))
        lks.append(lk if vis is None else jnp.where(vis, lk, 0.0))
    css = []
    for lk in lks:
        hi = lk.astype(BF16)
        lo = (lk - hi.astype(F32)).astype(BF16)
        css.append(_dot(hi, u2) + _dot(lo, u2))
    probs = []
    for z, lk, cs, c_ref in zip(zs, lks, css, c_refs):
        c = c_ref[...]
        a = jnp.exp(z + lk + cs[:, :tk] + c)
        probs.append((a if vis is None else jnp.where(vis, a, 0.0)).astype(BF16))
        c_ref[...] = c + cs[:, tk:]
    for a, v, acc_ref in zip(probs, vs, acc_refs):
        acc_ref[...] += _dot(a, v)


def _sb_prompt_kernel(q_ref, k_ref, v_ref, o_ref, acc_ref, c_ref, *, tq, dh):
    tk = SB_TK
    i = pl.program_id(1)
    nd = tq // tk
    heads = q_ref.shape[1] // dh
    acc_ref[...] = jnp.zeros_like(acc_ref)
    c_ref[...] = jnp.zeros_like(c_ref)
    u2 = _sb_consts(tk)
    row = lax.broadcasted_iota(jnp.int32, (tq, tk), 0)
    col = lax.broadcasted_iota(jnp.int32, (tq, tk), 1)

    def chunk(start, vis):
        sls = [slice(h * dh, (h + 1) * dh) for h in range(heads)]
        _sb_chunk([q_ref[:, sl] for sl in sls], [k_ref[pl.ds(start, tk), sl] for sl in sls],
                  [v_ref[pl.ds(start, tk), sl] for sl in sls], u2,
                  [acc_ref.at[h] for h in range(heads)], [c_ref.at[h] for h in range(heads)], vis)

    for d in range(nd - 1, -1, -1):
        chunk(pl.multiple_of(i * tq + d * tk, tk), (col + d * tk) < row)

    def live():
        return (jnp.max(c_ref[...]) > SB_EXIT).astype(jnp.int32)

    def cond(s):
        return jnp.logical_and(s[0] >= 0, s[1] > 0)

    def body(s):
        chunk(pl.multiple_of(s[0] * tk, tk), None)
        return s[0] - 1, live()

    lax.while_loop(cond, body, (i * nd - 1, live()))
    for h in range(heads):
        o_ref[:, h * dh:(h + 1) * dh] = acc_ref[h].astype(o_ref.dtype)


def _sb_prompt(q, k, v, n_heads, dh):
    t = q.shape[0]
    tq = min(t, 256)
    hs = SB_HEADS_PER_STEP
    assert n_heads % hs == 0
    w = hs * dh
    return pl.pallas_call(
        functools.partial(_sb_prompt_kernel, tq=tq, dh=dh),
        out_shape=jax.ShapeDtypeStruct((t, n_heads * dh), BF16),
        grid=(n_heads // hs, t // tq),
        in_specs=[pl.BlockSpec((tq, w), lambda h, i: (i, h)),
                  pl.BlockSpec((t, w), lambda h, i: (0, h)),
                  pl.BlockSpec((t, w), lambda h, i: (0, h))],
        out_specs=pl.BlockSpec((tq, w), lambda h, i: (i, h)),
        scratch_shapes=[pltpu.VMEM((hs, tq, dh), F32), pltpu.VMEM((hs, tq, LANES), F32)],
        compiler_params=_params(("parallel", "arbitrary")),
        name="sb_prompt",
    )(q, k, v)


def _softmax_steps(zs, vs, stats):
    alphas, ps = [], []
    for z, (m_ref, l_ref, _) in zip(zs, stats):
        m_old = m_ref[...]
        m_new = jnp.maximum(m_old, jnp.max(z, axis=1, keepdims=True))
        alpha = jnp.exp(m_old - m_new)
        p = jnp.exp(z - m_new)
        l_ref[...] = alpha * l_ref[...] + jnp.sum(p, axis=1, keepdims=True)
        m_ref[...] = m_new
        alphas.append(alpha)
        ps.append(p.astype(BF16))
    for alpha, p, v, (_, _, acc_ref) in zip(alphas, ps, vs, stats):
        acc_ref[...] = alpha * acc_ref[...] + _dot(p, v)


def _df_prompt_kernel(lam_ref, q_ref, k_ref, vt_ref, tiles_ref, g_ref, o_ref,
                      m_ref, l_ref, acc_ref, *, dh, out_scale):
    blk = DF_BLK
    i = pl.program_id(1)
    m_ref[...] = jnp.full_like(m_ref, NEG)
    l_ref[...] = jnp.zeros_like(l_ref)
    acc_ref[...] = jnp.zeros_like(acc_ref)
    q = q_ref[...]

    def step(kb, bias_t, nblk=1):
        start = pl.multiple_of(kb * nblk * blk, nblk * blk)
        ks = k_ref[pl.ds(start, nblk * blk), :]
        vt = vt_ref[:, pl.ds(start, nblk * blk)]
        zts = [_dot_nt(ks[:, c * dh:(c + 1) * dh], q[:, c * dh:(c + 1) * dh]) for c in range(2)]
        if bias_t is not None:
            zts = [zt + bias_t for zt in zts]
        alphas, ps = [], []
        for c, zt in enumerate(zts):
            m_old = m_ref[c]
            m_new = jnp.maximum(m_old, jnp.max(zt, axis=0, keepdims=True))
            alpha = jnp.exp2(m_old - m_new)
            p = jnp.exp2(zt - m_new)
            l_ref[c] = alpha * l_ref[c] + jnp.sum(p, axis=0, keepdims=True)
            m_ref[c] = m_new
            alphas.append(alpha)
            ps.append(p.astype(BF16))
        for c in range(2):
            acc_ref[c] = alphas[c] * acc_ref[c] + _dot(vt, ps[c])

    n_far = jnp.maximum(i - 1, 0)

    def far_body(kb, carry):
        step(kb, None, DF_FAR_BLOCKS)
        return carry

    lax.fori_loop(0, n_far // DF_FAR_BLOCKS, far_body, 0)

    def rest_body(kb, carry):
        step(kb, None)
        return carry

    lax.fori_loop(n_far // DF_FAR_BLOCKS * DF_FAR_BLOCKS, n_far, rest_body, 0)

    @pl.when(i >= 1)
    def _():
        step(i - 1, tiles_ref[1])

    step(i, tiles_ref[0])
    o_t = acc_ref[0] / l_ref[0] - lam_ref[0] * (acc_ref[1] / l_ref[1])
    o_ref[...] = (_rms(o_t.T, g_ref[...]) * out_scale).astype(o_ref.dtype)


def _df_prompt(q, k, v_t, tiles_t, lam, g, n_heads, dh, out_scale):
    t = q.shape[0]
    blk = DF_BLK
    dv = 2 * dh
    assert t % blk == 0
    grid_spec = pltpu.PrefetchScalarGridSpec(
        num_scalar_prefetch=1,
        grid=(n_heads, t // blk),
        in_specs=[pl.BlockSpec((blk, dv), lambda h, i, *_: (i, h)),
                  pl.BlockSpec((t, dv), lambda h, i, *_: (0, h)),
                  pl.BlockSpec((dv, t), lambda h, i, *_: (h, 0)),
                  pl.BlockSpec((None, 2, blk, blk), lambda h, i, *_: (h, 0, 0, 0)),
                  pl.BlockSpec((1, dv), lambda h, i, *_: (0, 0))],
        out_specs=pl.BlockSpec((blk, dv), lambda h, i, *_: (i, h)),
        scratch_shapes=[pltpu.VMEM((2, 1, blk), F32), pltpu.VMEM((2, 1, blk), F32),
                        pltpu.VMEM((2, dv, blk), F32)],
    )
    return pl.pallas_call(
        functools.partial(_df_prompt_kernel, dh=dh, out_scale=out_scale),
        out_shape=jax.ShapeDtypeStruct((t, n_heads * dv), BF16),
        grid_spec=grid_spec,
        compiler_params=_params(("parallel", "arbitrary")),
        name="df_prompt",
    )(lam, q, k, v_t, tiles_t, g.reshape(1, dv))


def _df_decode_kernel(pt_ref, far_ref, lam_ref, q_ref, blast_ref, bnew_ref, g_ref, knew_ref, vnew_ref, *rest,
                      n_heads, dh, ts, out_scale):
    pg = PAGES_PER_STEP
    k_refs, v_refs = rest[:pg], rest[pg:2 * pg]
    o_ref, kbf, vbf, m_ref, l_ref, acc_ref = rest[2 * pg:]
    dv = 2 * dh
    s = pl.program_id(1)
    last = s == pl.num_programs(1) - 1
    page = kbf.shape[0] // pg

    @pl.when(s == 0)
    def _():
        m_ref[...] = jnp.full_like(m_ref, NEG)
        l_ref[...] = jnp.zeros_like(l_ref)
        acc_ref[...] = jnp.zeros_like(acc_ref)

    for r in range(pg):
        rows_r = slice(r * page, (r + 1) * page)
        for o in range(2 * n_heads):
            kbf[rows_r, o * dh:(o + 1) * dh] = k_refs[r][pl.ds(o, page, stride=2 * n_heads), :].astype(BF16)
            j, h = divmod(o, n_heads)
            vbf[rows_r, (2 * h + j) * dh:(2 * h + j + 1) * dh] = (
                v_refs[r][pl.ds(o, page, stride=2 * n_heads), :].astype(BF16))
    heads = range(n_heads)
    stats = [(m_ref.at[h], l_ref.at[h], acc_ref.at[h]) for h in heads]
    zs = [_dot_nt(q_ref[h], kbf[:, h * dv:(h + 1) * dv]) + jnp.where(last, blast_ref[h], far_ref[h]) for h in heads]
    _softmax_steps(zs, [vbf[:, h * dv:(h + 1) * dv] for h in heads], stats)

    @pl.when(last)
    def _():
        pad = jnp.zeros((page - knew_ref.shape[0], dv), F32)
        kns = [jnp.concatenate([knew_ref[:, h * dv:(h + 1) * dv], pad], axis=0).astype(BF16) for h in heads]
        vns = [jnp.concatenate([vnew_ref[:, h * dv:(h + 1) * dv], pad], axis=0).astype(BF16) for h in heads]
        _softmax_steps([_dot_nt(q_ref[h], kns[h]) + bnew_ref[h] for h in heads], vns, stats)
        for h in heads:
            on = acc_ref[h] / l_ref[h]
            o = on - lam_ref[0] * pltpu.roll(on, on.shape[0] - ts, 0)
            o_ref[:, h * dv:(h + 1) * dv] = _rms(o, g_ref[...]) * out_scale


def _df_decode(page_table, far, lam, qbd, blast, bnew, g, knew, vnew, kcache, vcache, n_heads, dh, ts, out_scale):
    nb, n_pages = page_table.shape
    pg = PAGES_PER_STEP
    assert n_pages % pg == 0
    dv = 2 * dh
    width = n_heads * dv
    page = kcache.shape[1] * dh // width
    rows = qbd.shape[2]

    def page_spec(r):
        return pl.BlockSpec((None,) + kcache.shape[1:], lambda b, s, pt, *_: (pt[b * n_pages + s * pg + r], 0, 0))

    grid_spec = pltpu.PrefetchScalarGridSpec(
        num_scalar_prefetch=3,
        grid=(nb, n_pages // pg),
        in_specs=[pl.BlockSpec((None, n_heads, rows, dv), lambda b, s, *_: (b, 0, 0, 0)),
                  pl.BlockSpec((n_heads, rows, pg * page), lambda b, s, *_: (0, 0, 0)),
                  pl.BlockSpec((n_heads, rows, page), lambda b, s, *_: (0, 0, 0)),
                  pl.BlockSpec((1, dv), lambda b, s, *_: (0, 0)),
                  pl.BlockSpec((None, SUBLANES, width), lambda b, s, *_: (b, 0, 0)),
                  pl.BlockSpec((None, SUBLANES, width), lambda b, s, *_: (b, 0, 0))]
                 + [page_spec(r) for r in range(pg)] + [page_spec(r) for r in range(pg)],
        out_specs=pl.BlockSpec((None, rows, width), lambda b, s, *_: (b, 0, 0)),
        scratch_shapes=[pltpu.VMEM((pg * page, width), BF16), pltpu.VMEM((pg * page, width), BF16),
                        pltpu.VMEM((n_heads, rows, 1), F32), pltpu.VMEM((n_heads, rows, 1), F32),
                        pltpu.VMEM((n_heads, rows, dv), F32)],
    )
    return pl.pallas_call(
        functools.partial(_df_decode_kernel, n_heads=n_heads, dh=dh, ts=ts, out_scale=out_scale),
        out_shape=jax.ShapeDtypeStruct((nb, rows, width), F32),
        grid_spec=grid_spec,
        compiler_params=_params(("parallel", "arbitrary")),
        name="df_decode",
    )(page_table.reshape(-1), far, lam, qbd, blast, bnew, g.reshape(1, dv), knew, vnew,
      *([kcache] * pg), *([vcache] * pg))


def _sb_decode_kernel(pt_ref, q_ref, knew_ref, vnew_ref, kc_ref, vc_ref, o_ref, kbuf, vbuf, sem, acc_ref, c_ref,
                      *, n_heads, dh, ts, n_pages):
    b = pl.program_id(0)
    page = kbuf.shape[1] // n_heads
    rows = q_ref.shape[1]
    u2 = _sb_consts(page)

    def page_copies(p, slot):
        pid = pt_ref[b * n_pages + p]
        return (pltpu.make_async_copy(kc_ref.at[pid], kbuf.at[slot], sem.at[0, slot]),
                pltpu.make_async_copy(vc_ref.at[pid], vbuf.at[slot], sem.at[1, slot]))

    def start(p, slot):
        for cp in page_copies(p, slot):
            cp.start()

    def wait(p, slot):
        for cp in page_copies(p, slot):
            cp.wait()

    start(n_pages - 1, (n_pages - 1) % 2)

    acc_ref[...] = jnp.zeros_like(acc_ref)
    c_ref[...] = jnp.zeros_like(c_ref)
    row = lax.broadcasted_iota(jnp.int32, (rows, page), 0)
    col = lax.broadcasted_iota(jnp.int32, (rows, page), 1)
    vis = col < jnp.minimum(row, ts)
    pad = jnp.zeros((page - knew_ref.shape[0], dh), F32)
    heads = range(n_heads)
    qs = [q_ref[h] for h in heads]
    acc_refs = [acc_ref.at[h] for h in heads]
    c_refs = [c_ref.at[h] for h in heads]
    _sb_chunk(qs,
              [jnp.concatenate([knew_ref[:, h * dh:(h + 1) * dh], pad], axis=0).astype(BF16) for h in heads],
              [jnp.concatenate([vnew_ref[:, h * dh:(h + 1) * dh], pad], axis=0).astype(BF16) for h in heads],
              u2, acc_refs, c_refs, vis)

    def live():
        return (jnp.max(c_ref[...]) > SB_EXIT).astype(jnp.int32)

    def cond(s):
        return jnp.logical_and(s[0] >= 0, s[1] > 0)

    def body(s):
        p = s[0]
        slot = lax.rem(p, 2)
        wait(p, slot)

        @pl.when(p >= 1)
        def _():
            start(p - 1, 1 - slot)

        _sb_chunk(qs,
                  [kbuf[slot, pl.ds(h, page, stride=n_heads), :].astype(BF16) for h in heads],
                  [vbuf[slot, pl.ds(h, page, stride=n_heads), :].astype(BF16) for h in heads],
                  u2, acc_refs, c_refs, None)
        return p - 1, live()

    p_end, _ = lax.while_loop(cond, body, (n_pages - 1, live()))

    @pl.when(p_end >= 0)
    def _():
        wait(p_end, lax.rem(p_end, 2))

    for h in range(n_heads):
        o_ref[:, h * dh:(h + 1) * dh] = acc_ref[h]


def _sb_decode(page_table, q, knew, vnew, kcache, vcache, n_heads, dh, ts):
    nb, n_pages = page_table.shape
    width = n_heads * dh
    rows = q.shape[2]
    page_rows = kcache.shape[1]
    grid_spec = pltpu.PrefetchScalarGridSpec(
        num_scalar_prefetch=1,
        grid=(nb,),
        in_specs=[pl.BlockSpec((None, n_heads, rows, dh), lambda b, pt: (b, 0, 0, 0)),
                  pl.BlockSpec((None, SUBLANES, width), lambda b, pt: (b, 0, 0)),
                  pl.BlockSpec((None, SUBLANES, width), lambda b, pt: (b, 0, 0)),
                  pl.BlockSpec(memory_space=pl.ANY),
                  pl.BlockSpec(memory_space=pl.ANY)],
        out_specs=pl.BlockSpec((None, rows, width), lambda b, pt: (b, 0, 0)),
        scratch_shapes=[pltpu.VMEM((2, page_rows, dh), F32), pltpu.VMEM((2, page_rows, dh), F32),
                        pltpu.SemaphoreType.DMA((2, 2)),
                        pltpu.VMEM((n_heads, rows, dh), F32), pltpu.VMEM((n_heads, rows, LANES), F32)],
    )
    return pl.pallas_call(
        functools.partial(_sb_decode_kernel, n_heads=n_heads, dh=dh, ts=ts, n_pages=n_pages),
        out_shape=jax.ShapeDtypeStruct((nb, rows, width), F32),
        grid_spec=grid_spec,
        compiler_params=_params(("arbitrary",)),
        name="sb_decode",
    )(page_table.reshape(-1), q, knew, vnew, kcache, vcache)


def _merge_kernel(osb_ref, odf_ref, gsb_ref, gdf_ref, x_ref, wsb_ref, wdf_ref, wout_ref, g_ref, x1_ref, h_ref):
    y = (gsb_ref[...].astype(F32) * _dot(osb_ref[...], wsb_ref[...])
         + gdf_ref[...].astype(F32) * _dot(odf_ref[...], wdf_ref[...]))
    x1 = x_ref[...] + _dot(y.astype(BF16), wout_ref[...])
    x1_ref[...] = x1
    h_ref[...] = _rms(x1, g_ref[...]).astype(BF16)


def _merge(o_sb, o_df, gates, x, w_sb_o, w_diff_o, w_out, norm_g):
    m, d = x.shape
    w1 = o_sb.shape[1]
    tm = min(m, 256)
    const = lambda i: (0, 0)
    return pl.pallas_call(
        _merge_kernel,
        out_shape=[jax.ShapeDtypeStruct((m, d), F32), jax.ShapeDtypeStruct((m, d), BF16)],
        grid=(m // tm,),
        in_specs=[pl.BlockSpec((tm, w1), lambda i: (i, 0)),
                  pl.BlockSpec((tm, w1), lambda i: (i, 0)),
                  pl.BlockSpec((tm, d), lambda i: (i, 0)),
                  pl.BlockSpec((tm, d), lambda i: (i, 1)),
                  pl.BlockSpec((tm, d), lambda i: (i, 0)),
                  pl.BlockSpec((w1, d), const, pipeline_mode=pl.Buffered(1)),
                  pl.BlockSpec((w1, d), const, pipeline_mode=pl.Buffered(1)),
                  pl.BlockSpec((d, d), const, pipeline_mode=pl.Buffered(1)),
                  pl.BlockSpec((1, d), const)],
        out_specs=[pl.BlockSpec((tm, d), lambda i: (i, 0)), pl.BlockSpec((tm, d), lambda i: (i, 0))],
        compiler_params=_params(("parallel",)),
        name="merge",
    )(o_sb, o_df, gates, gates, x, w_sb_o, w_diff_o, w_out, norm_g.reshape(1, d))


def _mem_attn_kernel(q_ref, mk_ref, mv_ref, o_ref, *, n_heads, dh):
    interleaved = mk_ref.shape[1] == dh
    n_mem = mk_ref.shape[0] // n_heads if interleaved else mk_ref.shape[0]
    for h in range(n_heads):
        sl = slice(h * dh, (h + 1) * dh)
        if interleaved:
            mk = mk_ref[pl.ds(h, n_mem, stride=n_heads), :]
            mv = mv_ref[pl.ds(h, n_mem, stride=n_heads), :]
        else:
            mk, mv = mk_ref[:, sl], mv_ref[:, sl]
        z = _dot_nt(q_ref[:, sl], mk.astype(BF16))
        p = jnp.exp(z - jnp.max(z, axis=1, keepdims=True))
        o = _dot(p.astype(BF16), mv.astype(BF16)) / jnp.sum(p, axis=1, keepdims=True)
        o_ref[:, sl] = o.astype(o_ref.dtype)


def _mem_attn(q, mk, mv, n_heads, dh):
    nb, m, w = q.shape
    tm = min(m, 512)
    return pl.pallas_call(
        functools.partial(_mem_attn_kernel, n_heads=n_heads, dh=dh),
        out_shape=jax.ShapeDtypeStruct((nb, m, w), BF16),
        grid=(nb, m // tm),
        in_specs=[pl.BlockSpec((None, tm, w), lambda b, i: (b, i, 0)),
                  pl.BlockSpec((None,) + mk.shape[1:], lambda b, i: (b, 0, 0)),
                  pl.BlockSpec((None,) + mv.shape[1:], lambda b, i: (b, 0, 0))],
        out_specs=pl.BlockSpec((None, tm, w), lambda b, i: (b, i, 0)),
        compiler_params=_params(("parallel", "arbitrary")),
        name="mem_attn",
    )(q, mk, mv)


def _ffn_kernel(*refs, sample, ts, tail):
    if sample:
        h_ref, p1_ref, p2_ref = refs[:3]
        pos = 3
    else:
        h_ref, halo_ref = refs[:2]
        pos = 2
    (x_ref, wg_ref, wu_ref, wd_ref, cw_ref, cb_ref, gf_ref, y_ref, gout_ref, acc_ref) = refs[pos:]
    i = pl.program_id(0)
    f = pl.program_id(1)
    tm, tf = h_ref.shape[0], wg_ref.shape[1]

    @pl.when(f == 0)
    def _():
        acc_ref[...] = jnp.zeros_like(acc_ref)

    h = h_ref[...]
    g = _dot(h, wg_ref[...])
    u = _dot(h, wu_ref[...])
    g1 = pltpu.roll(g, 1, 0)
    g2 = pltpu.roll(g, 2, 0)
    if sample:
        gout_ref[...] = g
        t = lax.rem(lax.broadcasted_iota(jnp.int32, (tm, tf), 0), ts)
        g1 = jnp.where(t >= 1, g1, p1_ref[...])
        g2 = jnp.where(t >= 2, g2, p2_ref[...])
    else:
        gout_ref[...] = g[tm - tail:, :]
        gh = _dot(halo_ref[...], wg_ref[...]) * (i > 0).astype(F32)
        row = lax.broadcasted_iota(jnp.int32, (SUBLANES, tf), 0)
        top1 = jnp.where(row < 1, pltpu.roll(gh, 1, 0)[:SUBLANES], g1[:SUBLANES])
        top2 = jnp.where(row < 2, pltpu.roll(gh, 2, 0)[:SUBLANES], g2[:SUBLANES])
        g1 = jnp.concatenate([top1, g1[SUBLANES:]], axis=0)
        g2 = jnp.concatenate([top2, g2[SUBLANES:]], axis=0)
    cw = cw_ref[...]
    c = cb_ref[...] + cw[0:1] * g2 + cw[1:2] * g1 + cw[2:3] * g
    a = c / (1.0 + jnp.exp(-c)) * u
    acc_ref[...] += _dot(a.astype(BF16), wd_ref[...])

    @pl.when(f == pl.num_programs(1) - 1)
    def _():
        y_ref[...] = _rms(x_ref[...] + acc_ref[...], gf_ref[...])


def _ffn(h, x, wg, wu, wd, cw, cb, norm_f, *, prev=None, ts=1, tf=512):
    m, d = x.shape
    fp = wg.shape[1]
    sample = prev is not None
    tm = min(m, 512)
    tail = SUBLANES
    assert fp % tf == 0 and m % tm == 0
    row = lambda i, f: (i, 0)
    in_specs = [pl.BlockSpec((tm, d), row)]
    args = [h]
    if sample:
        in_specs += [pl.BlockSpec((tm, tf), lambda i, f: (i, f))] * 2
        args += list(prev)
        g_shape, g_spec = (m, fp), pl.BlockSpec((tm, tf), lambda i, f: (i, f))
    else:
        halo_blocks = tm // HALO
        in_specs.append(pl.BlockSpec((HALO, d), lambda i, f: (jnp.maximum(i * halo_blocks - 1, 0), 0)))
        args.append(h)
        g_shape, g_spec = (m // tm * tail, fp), pl.BlockSpec((tail, tf), lambda i, f: (i, f))
    in_specs += [pl.BlockSpec((tm, d), row),
                 pl.BlockSpec((d, tf), lambda i, f: (0, f)),
                 pl.BlockSpec((d, tf), lambda i, f: (0, f)),
                 pl.BlockSpec((tf, d), lambda i, f: (f, 0)),
                 pl.BlockSpec((CONV_W, tf), lambda i, f: (0, f)),
                 pl.BlockSpec((1, tf), lambda i, f: (0, f)),
                 pl.BlockSpec((1, d), lambda i, f: (0, 0))]
    args += [x, wg, wu, wd, cw, cb.reshape(1, fp), norm_f.reshape(1, d)]
    return pl.pallas_call(
        functools.partial(_ffn_kernel, sample=sample, ts=ts, tail=tail),
        out_shape=[jax.ShapeDtypeStruct((m, d), F32), jax.ShapeDtypeStruct(g_shape, F32)],
        grid=(m // tm, fp // tf),
        in_specs=in_specs,
        out_specs=[pl.BlockSpec((tm, d), row), g_spec],
        scratch_shapes=[pltpu.VMEM((tm, d), F32)],
        compiler_params=_params(("arbitrary", "arbitrary")),
        name="conv_ffn",
    )(*args)


def _t5_bias_by_distance(t5_bias, n):
    d = jnp.arange(n, dtype=jnp.int32)
    max_exact = NUM_BUCKETS // 2
    df = jnp.maximum(d, 1).astype(F32)
    large = max_exact + (jnp.log(df / max_exact) / math.log(MAX_DISTANCE / max_exact)
                         * (NUM_BUCKETS - max_exact)).astype(jnp.int32)
    large = jnp.minimum(large, NUM_BUCKETS - 1)
    bucket = jnp.where(d < max_exact, d, large)
    return t5_bias.astype(F32)[bucket].T


def _toeplitz_tile(bias_d, blk, off):
    n = 2 * blk
    j = jnp.arange(n, dtype=jnp.int32)
    dist = off - jnp.where(j < blk, j, j - n)
    e = jnp.where(dist >= 0, bias_d[:, jnp.clip(dist, 0, bias_d.shape[1] - 1)], NEG)
    skew = jnp.tile(e, (1, blk))[:, :blk * (n - 1)].reshape(-1, blk, n - 1)
    return skew[:, :, :blk]


def _bias_rows(bias_d, toks, first_dist, n_keys, n_valid):
    rows = []
    for tok in toks:
        top = first_dist + tok
        n_ok = min(n_valid, top + 1)
        ok = jnp.flip(bias_d[:, top - n_ok + 1:top + 1], axis=1)
        rows.append(jnp.pad(ok, ((0, 0), (0, n_keys - n_ok)), constant_values=NEG))
    return jnp.stack(rows, axis=1)


def _pad_rows(a, rows):
    pad = [(0, 0)] * a.ndim
    pad[-2] = (0, rows - a.shape[-2])
    return jnp.pad(a, pad)


def kernel(x_prompt, x_sample, mem_prompt, cache_sb_k, cache_sb_v, cache_diff_k, cache_diff_v, cache_mem_k, cache_mem_v, state_conv, page_table, norm_mix, w_in, w_sb_o, w_diff_o, w_out, diff_subln_g, lambda_q1, lambda_k1, lambda_q2, lambda_k2, t5_bias, norm_cross, norm_mem, w_mq, w_mk, w_mv, w_mo, norm_ffn, w_gate, w_up, conv_w, conv_b, w_down, norm_f):
    depth = w_in.shape[0]
    assert depth == 1 and x_prompt.shape[0] == 1
    _, t, d = x_prompt.shape
    nb, ts, _ = x_sample.shape
    n_pool, page, h_sb, dh = cache_sb_k.shape[1:]
    h_df = cache_diff_k.shape[3]
    dv = 2 * dh
    h_mem = cache_mem_k.shape[3]
    n_mem = mem_prompt.shape[1]
    d_ff = w_gate.shape[2]
    n_pages = page_table.shape[1]
    past = n_pages * page
    sb_w = h_sb * dh
    scale = dh ** -0.5
    lam_init = 0.8 - 0.6 * math.exp(-0.3 * 0)
    out_scale = 1.0 - lam_init
    assert 2 * ts <= QROWS and ts >= CONV_W - 1 and page == LANES and DF_BLK >= MAX_DISTANCE

    lam = (jnp.exp(jnp.sum(lambda_q1[0].astype(F32) * lambda_k1[0].astype(F32)))
           - jnp.exp(jnp.sum(lambda_q2[0].astype(F32) * lambda_k2[0].astype(F32))) + lam_init).reshape(1)

    bias_d = _t5_bias_by_distance(t5_bias, max(2 * DF_BLK, PAGES_PER_STEP * page + QROWS))
    far = bias_d[:, MAX_DISTANCE]
    tiles = jnp.stack([_toeplitz_tile(bias_d, DF_BLK, 0), _toeplitz_tile(bias_d, DF_BLK, DF_BLK)], axis=1)

    tf = 512
    fp = ((d_ff + tf - 1) // tf) * tf
    w_in_b = w_in[0].astype(BF16)
    w_sb_o_b, w_diff_o_b, w_out_b = w_sb_o[0].astype(BF16), w_diff_o[0].astype(BF16), w_out[0].astype(BF16)
    w_mq_b, w_mk_b, w_mv_b, w_mo_b = (w[0].astype(BF16) for w in (w_mq, w_mk, w_mv, w_mo))
    wg_b = jnp.pad(w_gate[0].astype(BF16), ((0, 0), (0, fp - d_ff)))
    wu_b = jnp.pad(w_up[0].astype(BF16), ((0, 0), (0, fp - d_ff)))
    wd_b = jnp.pad(w_down[0].astype(BF16), ((0, fp - d_ff), (0, 0)))
    cw_p = jnp.pad(conv_w[0], ((0, 0), (0, fp - d_ff)))
    cb_p = jnp.pad(conv_b[0], ((0, fp - d_ff),))

    def project(x2d, attn_copies):
        h = _rmsnorm_bf16(x2d, norm_mix[0])
        kv = [F32, BF16] if attn_copies else [F32]
        c = 0
        q_sb = _linear(h, w_in_b, c, sb_w, [BF16], scale=scale)[0]; c += sb_w
        k_sb = _linear(h, w_in_b, c, sb_w, kv); c += sb_w
        v_sb = _linear(h, w_in_b, c, sb_w, kv); c += sb_w
        q_df = _linear(h, w_in_b, c, h_df * dv, [BF16], scale=scale * LOG2E if attn_copies else scale)[0]
        c += h_df * dv
        k_df = _linear(h, w_in_b, c, h_df * dv, kv); c += h_df * dv
        v_df = _linear(h, w_in_b, c, h_df * dv, [F32], transposed=attn_copies); c += h_df * dv
        gates = _linear(h, w_in_b, c, 2 * d, [BF16], sigmoid=True, tn=2048)[0]
        return q_sb, k_sb, v_sb, q_df, k_df, v_df, gates

    def tail(x2d, o_sb, o_df, gates, mk, mv, rows_per_mem, ffn_prev, ffn_ts):
        x1, h2 = _merge(o_sb, o_df, gates, x2d, w_sb_o_b, w_diff_o_b, w_out_b, norm_cross[0])
        qm, = _linear(h2, w_mq_b, 0, h_mem * dh, [BF16], scale=scale)
        nbm = mk.shape[0]
        qm = qm.reshape(nbm, rows_per_mem, h_mem * dh)
        if rows_per_mem < QROWS:
            om = _mem_attn(_pad_rows(qm, QROWS), mk, mv, h_mem, dh)[:, :rows_per_mem]
        else:
            om = _mem_attn(qm, mk, mv, h_mem, dh)
        om = om.reshape(nbm * rows_per_mem, h_mem * dh)
        x2, h3 = _linear(om, w_mo_b, 0, d, [F32], res=x1, norm_g=norm_ffn[0])
        return _ffn(h3, x2, wg_b, wu_b, wd_b, cw_p, cb_p, norm_f, prev=ffn_prev, ts=ffn_ts, tf=tf)

    xp = x_prompt.reshape(t, d)
    q_sb, (k_sb, k_sb_b), (v_sb, v_sb_b), q_df, (k_df, k_df_b), (v_df, v_df_t), gates = project(xp, True)
    o_sb = _sb_prompt(q_sb, k_sb_b, v_sb_b, h_sb, dh)
    tiles_t = (jnp.swapaxes(tiles, 2, 3) - far[:, None, None, None]) * LOG2E
    o_df = _df_prompt(q_df, k_df_b, v_df_t, tiles_t, lam, diff_subln_g[0], h_df, dh, out_scale)
    m_b = _rmsnorm_bf16(mem_prompt.reshape(n_mem, d), norm_mem[0])
    mk, = _linear(m_b, w_mk_b, 0, h_mem * dh, [F32])
    mv, = _linear(m_b, w_mv_b, 0, h_mem * dh, [F32])
    y_p, g_tail = tail(xp, o_sb, o_df, gates, mk[None], mv[None], t, None, 1)
    conv_p = g_tail[g_tail.shape[0] - (CONV_W - 1):, :d_ff]

    xs = x_sample.reshape(nb * ts, d)
    qs_sb, (ks_sb,), (vs_sb,), qs_df, (ks_df,), (vs_df,), gates_s = project(xs, False)
    q1 = _pad_rows(qs_sb.reshape(nb, ts, h_sb, dh).transpose(0, 2, 1, 3), QROWS)
    os_sb = _sb_decode(page_table, q1, _pad_rows(ks_sb.reshape(nb, ts, sb_w), SUBLANES),
                       _pad_rows(vs_sb.reshape(nb, ts, sb_w), SUBLANES),
                       cache_sb_k.reshape(n_pool, page * h_sb, dh), cache_sb_v.reshape(n_pool, page * h_sb, dh),
                       h_sb, dh, ts)[:, :ts].astype(BF16)
    q2 = qs_df.reshape(nb, ts, h_df, 2, dh).transpose(0, 2, 3, 1, 4)
    zq = jnp.zeros_like(q2[:, :, 0])
    qbd = jnp.concatenate([jnp.concatenate([q2[:, :, 0], zq], axis=-1),
                           jnp.concatenate([zq, q2[:, :, 1]], axis=-1)], axis=2)
    qbd = _pad_rows(qbd, QROWS)
    toks = list(range(ts)) * 2 + [0] * (QROWS - 2 * ts)
    n_last = PAGES_PER_STEP * page
    blast = _bias_rows(bias_d, toks, n_last, n_last, n_last)
    bnew = _bias_rows(bias_d, toks, 0, page, ts)
    kd2 = cache_diff_k.reshape(n_pool, page * h_df * 2, dh)
    vd2 = cache_diff_v.reshape(n_pool, page, h_df, 2, dh).transpose(0, 1, 3, 2, 4).reshape(n_pool, page * 2 * h_df, dh)
    os_df = _df_decode(page_table, far, lam, qbd, blast, bnew, diff_subln_g[0],
                       _pad_rows(ks_df.reshape(nb, ts, h_df * dv), SUBLANES),
                       _pad_rows(vs_df.reshape(nb, ts, h_df * dv), SUBLANES),
                       kd2, vd2, h_df, dh, ts, out_scale)[:, :ts].astype(BF16)
    st = jnp.pad(state_conv[0], ((0, 0), (0, 0), (0, fp - d_ff)))
    zero = jnp.zeros((nb, 1, fp), F32)
    p1 = jnp.concatenate([st[:, 1:2]] + [zero] * (ts - 1), axis=1).reshape(nb * ts, fp)
    p2 = jnp.concatenate([st[:, 0:1], st[:, 1:2]] + [zero] * (ts - 2), axis=1).reshape(nb * ts, fp)
    y_s, g_s = tail(xs, os_sb.reshape(nb * ts, sb_w), os_df.reshape(nb * ts, h_df * dv), gates_s,
                    cache_mem_k.reshape(nb, n_mem * h_mem, dh), cache_mem_v.reshape(nb, n_mem * h_mem, dh),
                    ts, (p1, p2), ts)
    gp = jnp.concatenate([state_conv[0], g_s[:, :d_ff].reshape(nb, ts, d_ff)], axis=1)
    conv_s = gp[:, ts:]

    return (y_p.reshape(1, t, d), y_s.reshape(nb, ts, d),
            k_sb.reshape(1, 1, t, h_sb, dh), v_sb.reshape(1, 1, t, h_sb, dh),
            k_df.reshape(1, 1, t, h_df, 2, dh), v_df.reshape(1, 1, t, h_df, dv),
            mk.reshape(1, 1, n_mem, h_mem, dh), mv.reshape(1, 1, n_mem, h_mem, dh),
            conv_p.reshape(1, 1, CONV_W - 1, d_ff),
            ks_sb.reshape(1, nb, ts, h_sb, dh), vs_sb.reshape(1, nb, ts, h_sb, dh),
            ks_df.reshape(1, nb, ts, h_df, 2, dh), vs_df.reshape(1, nb, ts, h_df, dv),
            conv_s.reshape(1, nb, CONV_W - 1, d_ff))
```

```python
import functools
import math

import jax
import jax.numpy as jnp
from jax import lax
from jax.experimental import pallas as pl
from jax.experimental.pallas import tpu as pltpu

F32 = jnp.float32
BF16 = jnp.bfloat16
EPS = 1e-6
NUM_BUCKETS = 32
MAX_DISTANCE = 128
CONV_W = 3
LANES = 128
SUBLANES = 8
VMEM_LIMIT = 56 * 1024 * 1024
NEG = -1e30
LOG2E = 1.4426950408889634
SB_EXIT = -120.0
SB_TK = 128
SB_HEADS_PER_STEP = 4
DF_BLK = 256
DF_FAR_BLOCKS = 4
PAGES_PER_STEP = 16
QROWS = 16
HALO = 16


def _params(sem):
    return pltpu.CompilerParams(dimension_semantics=sem, vmem_limit_bytes=VMEM_LIMIT)


def _dot(a, b):
    return jnp.dot(a, b, preferred_element_type=F32)


def _dot_nt(a, b):
    return lax.dot_general(a, b, (((1,), (1,)), ((), ())), preferred_element_type=F32)


def _rms(x, g):
    return x * lax.rsqrt(jnp.mean(x * x, axis=-1, keepdims=True) + EPS) * g


def _rms_kernel(x_ref, g_ref, o_ref):
    o_ref[...] = _rms(x_ref[...], g_ref[...]).astype(o_ref.dtype)


def _rmsnorm_bf16(x, g):
    m, d = x.shape
    tm = min(m, 512)
    return pl.pallas_call(
        _rms_kernel,
        out_shape=jax.ShapeDtypeStruct((m, d), BF16),
        grid=(m // tm,),
        in_specs=[pl.BlockSpec((tm, d), lambda i: (i, 0)), pl.BlockSpec((1, d), lambda i: (0, 0))],
        out_specs=pl.BlockSpec((tm, d), lambda i: (i, 0)),
        compiler_params=_params(("parallel",)),
        name="rmsnorm",
    )(x, g.reshape(1, d))


def _linear_kernel(*refs, scale, sigmoid, has_res, has_norm, n_plain, transposed, slots):
    x_ref, w_ref = refs[0], refs[1]
    pos = 2
    res_ref = g_ref = None
    if has_res:
        res_ref = refs[pos]
        pos += 1
    if has_norm:
        g_ref = refs[pos]
        pos += 1
    outs = refs[pos:]
    y = _dot(x_ref[...], w_ref[...])
    if scale is not None:
        y = y * scale
    if sigmoid:
        y = 1.0 / (1.0 + jnp.exp(-y))
    if has_res:
        y = y + res_ref[...]
    for k, o in enumerate(outs[:n_plain]):
        if k == 0 and slots is not None:
            for s, cb in enumerate(slots):
                o[pl.ds(s, y.shape[0], stride=len(slots)), :] = y[:, cb * LANES:(cb + 1) * LANES].astype(o.dtype)
        else:
            o[...] = y.astype(o.dtype)
    if has_norm:
        outs[n_plain][...] = _rms(y, g_ref[...]).astype(BF16)
    if transposed:
        outs[-1][...] = y.T.astype(BF16)


def _linear(x, w, col0, ncols, out_dtypes, *, scale=None, sigmoid=False, res=None, norm_g=None, tn=1024,
            transposed=False, slots=None):
    m, k = x.shape
    tm = min(m, 512)
    tn = min(tn, ncols)
    if norm_g is not None or slots is not None:
        tn = ncols
    assert m % tm == 0 and ncols % tn == 0 and col0 % tn == 0
    off = col0 // tn
    in_specs = [pl.BlockSpec((tm, k), lambda j, i: (i, 0)),
                pl.BlockSpec((k, tn), lambda j, i: (0, j + off))]
    args = [x, w]
    if res is not None:
        in_specs.append(pl.BlockSpec((tm, tn), lambda j, i: (i, j)))
        args.append(res)
    if norm_g is not None:
        in_specs.append(pl.BlockSpec((1, tn), lambda j, i: (0, 0)))
        args.append(norm_g.reshape(1, ncols))
    dts = list(out_dtypes) + ([BF16] if norm_g is not None else [])
    out_shape = [jax.ShapeDtypeStruct((m, ncols), dt) for dt in dts]
    out_specs = [pl.BlockSpec((tm, tn), lambda j, i: (i, j)) for _ in dts]
    if slots is not None:
        assert len(slots) * LANES == ncols
        out_shape[0] = jax.ShapeDtypeStruct((m * len(slots), LANES), dts[0])
        out_specs[0] = pl.BlockSpec((tm * len(slots), LANES), lambda j, i: (i, 0))
    if transposed:
        out_shape.append(jax.ShapeDtypeStruct((ncols, m), BF16))
        out_specs.append(pl.BlockSpec((tn, tm), lambda j, i: (j, i)))
    out = pl.pallas_call(
        functools.partial(_linear_kernel, scale=scale, sigmoid=sigmoid, has_res=res is not None,
                          has_norm=norm_g is not None, n_plain=len(out_dtypes), transposed=transposed,
                          slots=slots),
        out_shape=out_shape,
        grid=(ncols // tn, m // tm),
        in_specs=in_specs,
        out_specs=out_specs,
        compiler_params=_params(("parallel", "arbitrary")),
        name="linear",
    )(*args)
    return out


def _sb_consts(tk):
    r = lax.broadcasted_iota(jnp.int32, (tk, tk + LANES), 0)
    c = lax.broadcasted_iota(jnp.int32, (tk, tk + LANES), 1)
    return jnp.where((r > c) | (c >= tk), 1.0, 0.0).astype(BF16)


def _sb_chunk(qs, ks, vs, u2, acc_refs, c_refs, vis):
    tk = ks[0].shape[0]
    zs = [_dot_nt(q, k) for q, k in zip(qs, ks)]
    lks = []
    for z in zs:
        lk = -(jnp.maximum(z, 0.0) + jnp.log(1.0 + jnp.exp(-jnp.abs(z))))
        lks.append(lk if vis is None else jnp.where(vis, lk, 0.0))
    css = []
    for lk in lks:
        hi = lk.astype(BF16)
        lo = (lk - hi.astype(F32)).astype(BF16)
        css.append(_dot(hi, u2) + _dot(lo, u2))
    probs = []
    for z, lk, cs, c_ref in zip(zs, lks, css, c_refs):
        c = c_ref[...]
        a = jnp.exp(z + lk + cs[:, :tk] + c)
        probs.append((a if vis is None else jnp.where(vis, a, 0.0)).astype(BF16))
        c_ref[...] = c + cs[:, tk:]
    for a, v, acc_ref in zip(probs, vs, acc_refs):
        acc_ref[...] += _dot(a, v)


def _sb_prompt_kernel(q_ref, k_ref, v_ref, o_ref, acc_ref, c_ref, *, tq, dh):
    tk = SB_TK
    i = pl.program_id(1)
    nd = tq // tk
    heads = q_ref.shape[1] // dh
    acc_ref[...] = jnp.zeros_like(acc_ref)
    c_ref[...] = jnp.zeros_like(c_ref)
    u2 = _sb_consts(tk)
    row = lax.broadcasted_iota(jnp.int32, (tq, tk), 0)
    col = lax.broadcasted_iota(jnp.int32, (tq, tk), 1)

    def chunk(start, vis):
        sls = [slice(h * dh, (h + 1) * dh) for h in range(heads)]
        _sb_chunk([q_ref[:, sl] for sl in sls], [k_ref[pl.ds(start, tk), sl] for sl in sls],
                  [v_ref[pl.ds(start, tk), sl] for sl in sls], u2,
                  [acc_ref.at[h] for h in range(heads)], [c_ref.at[h] for h in range(heads)], vis)

    for d in range(nd - 1, -1, -1):
        chunk(pl.multiple_of(i * tq + d * tk, tk), (col + d * tk) < row)

    def live():
        return (jnp.max(c_ref[...]) > SB_EXIT).astype(jnp.int32)

    def cond(s):
        return jnp.logical_and(s[0] >= 0, s[1] > 0)

    def body(s):
        chunk(pl.multiple_of(s[0] * tk, tk), None)
        return s[0] - 1, live()

    lax.while_loop(cond, body, (i * nd - 1, live()))
    for h in range(heads):
        o_ref[:, h * dh:(h + 1) * dh] = acc_ref[h].astype(o_ref.dtype)


def _sb_prompt(q, k, v, n_heads, dh):
    t = q.shape[0]
    tq = min(t, 256)
    hs = SB_HEADS_PER_STEP
    assert n_heads % hs == 0
    w = hs * dh
    return pl.pallas_call(
        functools.partial(_sb_prompt_kernel, tq=tq, dh=dh),
        out_shape=jax.ShapeDtypeStruct((t, n_heads * dh), BF16),
        grid=(n_heads // hs, t // tq),
        in_specs=[pl.BlockSpec((tq, w), lambda h, i: (i, h)),
                  pl.BlockSpec((t, w), lambda h, i: (0, h)),
                  pl.BlockSpec((t, w), lambda h, i: (0, h))],
        out_specs=pl.BlockSpec((tq, w), lambda h, i: (i, h)),
        scratch_shapes=[pltpu.VMEM((hs, tq, dh), F32), pltpu.VMEM((hs, tq, LANES), F32)],
        compiler_params=_params(("parallel", "arbitrary")),
        name="sb_prompt",
    )(q, k, v)


def _softmax_steps(zs, vs, stats):
    alphas, ps = [], []
    for z, (m_ref, l_ref, _) in zip(zs, stats):
        m_old = m_ref[...]
        m_new = jnp.maximum(m_old, jnp.max(z, axis=1, keepdims=True))
        alpha = jnp.exp(m_old - m_new)
        p = jnp.exp(z - m_new)
        l_ref[...] = alpha * l_ref[...] + jnp.sum(p, axis=1, keepdims=True)
        m_ref[...] = m_new
        alphas.append(alpha)
        ps.append(p.astype(BF16))
    for alpha, p, v, (_, _, acc_ref) in zip(alphas, ps, vs, stats):
        acc_ref[...] = alpha * acc_ref[...] + _dot(p, v)


def _df_prompt_kernel(lam_ref, q_ref, k_ref, vt_ref, tiles_ref, g_ref, o_ref,
                      m_ref, l_ref, acc_ref, *, dh, out_scale):
    blk = DF_BLK
    i = pl.program_id(1)
    m_ref[...] = jnp.full_like(m_ref, NEG)
    l_ref[...] = jnp.zeros_like(l_ref)
    acc_ref[...] = jnp.zeros_like(acc_ref)
    q = q_ref[...]

    def step(kb, bias_t, nblk=1):
        start = pl.multiple_of(kb * nblk * blk, nblk * blk)
        ks = k_ref[pl.ds(start, nblk * blk), :]
        vt = vt_ref[:, pl.ds(start, nblk * blk)]
        zts = [_dot_nt(ks[:, c * dh:(c + 1) * dh], q[:, c * dh:(c + 1) * dh]) for c in range(2)]
        if bias_t is not None:
            zts = [zt + bias_t for zt in zts]
        alphas, ps = [], []
        for c, zt in enumerate(zts):
            m_old = m_ref[c]
            m_new = jnp.maximum(m_old, jnp.max(zt, axis=0, keepdims=True))
            alpha = jnp.exp2(m_old - m_new)
            p = jnp.exp2(zt - m_new)
            l_ref[c] = alpha * l_ref[c] + jnp.sum(p, axis=0, keepdims=True)
            m_ref[c] = m_new
            alphas.append(alpha)
            ps.append(p.astype(BF16))
        for c in range(2):
            acc_ref[c] = alphas[c] * acc_ref[c] + _dot(vt, ps[c])

    n_far = jnp.maximum(i - 1, 0)

    def far_body(kb, carry):
        step(kb, None, DF_FAR_BLOCKS)
        return carry

    lax.fori_loop(0, n_far // DF_FAR_BLOCKS, far_body, 0)

    def rest_body(kb, carry):
        step(kb, None)
        return carry

    lax.fori_loop(n_far // DF_FAR_BLOCKS * DF_FAR_BLOCKS, n_far, rest_body, 0)

    @pl.when(i >= 1)
    def _():
        step(i - 1, tiles_ref[1])

    step(i, tiles_ref[0])
    o_t = acc_ref[0] / l_ref[0] - lam_ref[0] * (acc_ref[1] / l_ref[1])
    o_ref[...] = (_rms(o_t.T, g_ref[...]) * out_scale).astype(o_ref.dtype)


def _df_prompt(q, k, v_t, tiles_t, lam, g, n_heads, dh, out_scale):
    t = q.shape[0]
    blk = DF_BLK
    dv = 2 * dh
    assert t % blk == 0
    grid_spec = pltpu.PrefetchScalarGridSpec(
        num_scalar_prefetch=1,
        grid=(n_heads, t // blk),
        in_specs=[pl.BlockSpec((blk, dv), lambda h, i, *_: (i, h)),
                  pl.BlockSpec((t, dv), lambda h, i, *_: (0, h)),
                  pl.BlockSpec((dv, t), lambda h, i, *_: (h, 0)),
                  pl.BlockSpec((None, 2, blk, blk), lambda h, i, *_: (h, 0, 0, 0)),
                  pl.BlockSpec((1, dv), lambda h, i, *_: (0, 0))],
        out_specs=pl.BlockSpec((blk, dv), lambda h, i, *_: (i, h)),
        scratch_shapes=[pltpu.VMEM((2, 1, blk), F32), pltpu.VMEM((2, 1, blk), F32),
                        pltpu.VMEM((2, dv, blk), F32)],
    )
    return pl.pallas_call(
        functools.partial(_df_prompt_kernel, dh=dh, out_scale=out_scale),
        out_shape=jax.ShapeDtypeStruct((t, n_heads * dv), BF16),
        grid_spec=grid_spec,
        compiler_params=_params(("parallel", "arbitrary")),
        name="df_prompt",
    )(lam, q, k, v_t, tiles_t, g.reshape(1, dv))


def _df_decode_kernel(pt_ref, far_ref, lam_ref, q_ref, blast_ref, bnew_ref, g_ref, knew_ref, vnew_ref, *rest,
                      n_heads, dh, ts, out_scale):
    pg = PAGES_PER_STEP
    k_refs, v_refs = rest[:pg], rest[pg:2 * pg]
    o_ref, kbf, vbf, m_ref, l_ref, acc_ref = rest[2 * pg:]
    dv = 2 * dh
    s = pl.program_id(1)
    last = s == pl.num_programs(1) - 1
    page = kbf.shape[0] // pg

    @pl.when(s == 0)
    def _():
        m_ref[...] = jnp.full_like(m_ref, NEG)
        l_ref[...] = jnp.zeros_like(l_ref)
        acc_ref[...] = jnp.zeros_like(acc_ref)

    for r in range(pg):
        rows_r = slice(r * page, (r + 1) * page)
        for o in range(2 * n_heads):
            kbf[rows_r, o * dh:(o + 1) * dh] = k_refs[r][pl.ds(o, page, stride=2 * n_heads), :].astype(BF16)
            j, h = divmod(o, n_heads)
            vbf[rows_r, (2 * h + j) * dh:(2 * h + j + 1) * dh] = (
                v_refs[r][pl.ds(o, page, stride=2 * n_heads), :].astype(BF16))
    heads = range(n_heads)
    stats = [(m_ref.at[h], l_ref.at[h], acc_ref.at[h]) for h in heads]
    zs = [_dot_nt(q_ref[h], kbf[:, h * dv:(h + 1) * dv]) + jnp.where(last, blast_ref[h], far_ref[h]) for h in heads]
    _softmax_steps(zs, [vbf[:, h * dv:(h + 1) * dv] for h in heads], stats)

    @pl.when(last)
    def _():
        pad = jnp.zeros((page - knew_ref.shape[0], dv), F32)
        kns = [jnp.concatenate([knew_ref[:, h * dv:(h + 1) * dv], pad], axis=0).astype(BF16) for h in heads]
        vns = [jnp.concatenate([vnew_ref[:, h * dv:(h + 1) * dv], pad], axis=0).astype(BF16) for h in heads]
        _softmax_steps([_dot_nt(q_ref[h], kns[h]) + bnew_ref[h] for h in heads], vns, stats)
        for h in heads:
            on = acc_ref[h] / l_ref[h]
            o = on - lam_ref[0] * pltpu.roll(on, on.shape[0] - ts, 0)
            o_ref[:, h * dv:(h + 1) * dv] = _rms(o, g_ref[...]) * out_scale


def _df_decode(page_table, far, lam, qbd, blast, bnew, g, knew, vnew, kcache, vcache, n_heads, dh, ts, out_scale):
    nb, n_pages = page_table.shape
    pg = PAGES_PER_STEP
    assert n_pages % pg == 0
    dv = 2 * dh
    width = n_heads * dv
    page = kcache.shape[1] * dh // width
    rows = qbd.shape[2]

    def page_spec(r):
        return pl.BlockSpec((None,) + kcache.shape[1:], lambda b, s, pt, *_: (pt[b * n_pages + s * pg + r], 0, 0))

    grid_spec = pltpu.PrefetchScalarGridSpec(
        num_scalar_prefetch=3,
        grid=(nb, n_pages // pg),
        in_specs=[pl.BlockSpec((None, n_heads, rows, dv), lambda b, s, *_: (b, 0, 0, 0)),
                  pl.BlockSpec((n_heads, rows, pg * page), lambda b, s, *_: (0, 0, 0)),
                  pl.BlockSpec((n_heads, rows, page), lambda b, s, *_: (0, 0, 0)),
                  pl.BlockSpec((1, dv), lambda b, s, *_: (0, 0)),
                  pl.BlockSpec((None, SUBLANES, width), lambda b, s, *_: (b, 0, 0)),
                  pl.BlockSpec((None, SUBLANES, width), lambda b, s, *_: (b, 0, 0))]
                 + [page_spec(r) for r in range(pg)] + [page_spec(r) for r in range(pg)],
        out_specs=pl.BlockSpec((None, rows, width), lambda b, s, *_: (b, 0, 0)),
        scratch_shapes=[pltpu.VMEM((pg * page, width), BF16), pltpu.VMEM((pg * page, width), BF16),
                        pltpu.VMEM((n_heads, rows, 1), F32), pltpu.VMEM((n_heads, rows, 1), F32),
                        pltpu.VMEM((n_heads, rows, dv), F32)],
    )
    return pl.pallas_call(
        functools.partial(_df_decode_kernel, n_heads=n_heads, dh=dh, ts=ts, out_scale=out_scale),
        out_shape=jax.ShapeDtypeStruct((nb, rows, width), F32),
        grid_spec=grid_spec,
        compiler_params=_params(("parallel", "arbitrary")),
        name="df_decode",
    )(page_table.reshape(-1), far, lam, qbd, blast, bnew, g.reshape(1, dv), knew, vnew,
      *([kcache] * pg), *([vcache] * pg))


def _sb_decode_kernel(pt_ref, q_ref, knew_ref, vnew_ref, kc_ref, vc_ref, o_ref, kbuf, vbuf, sem, acc_ref, c_ref,
                      *, n_heads, dh, ts, n_pages):
    b = pl.program_id(0)
    page = kbuf.shape[1] // n_heads
    rows = q_ref.shape[1]
    u2 = _sb_consts(page)

    def page_copies(p, slot):
        pid = pt_ref[b * n_pages + p]
        return (pltpu.make_async_copy(kc_ref.at[pid], kbuf.at[slot], sem.at[0, slot]),
                pltpu.make_async_copy(vc_ref.at[pid], vbuf.at[slot], sem.at[1, slot]))

    def start(p, slot):
        for cp in page_copies(p, slot):
            cp.start()

    def wait(p, slot):
        for cp in page_copies(p, slot):
            cp.wait()

    start(n_pages - 1, (n_pages - 1) % 2)

    acc_ref[...] = jnp.zeros_like(acc_ref)
    c_ref[...] = jnp.zeros_like(c_ref)
    row = lax.broadcasted_iota(jnp.int32, (rows, page), 0)
    col = lax.broadcasted_iota(jnp.int32, (rows, page), 1)
    vis = col < jnp.minimum(row, ts)
    pad = jnp.zeros((page - knew_ref.shape[0], dh), F32)
    heads = range(n_heads)
    qs = [q_ref[h] for h in heads]
    acc_refs = [acc_ref.at[h] for h in heads]
    c_refs = [c_ref.at[h] for h in heads]
    _sb_chunk(qs,
              [jnp.concatenate([knew_ref[:, h * dh:(h + 1) * dh], pad], axis=0).astype(BF16) for h in heads],
              [jnp.concatenate([vnew_ref[:, h * dh:(h + 1) * dh], pad], axis=0).astype(BF16) for h in heads],
              u2, acc_refs, c_refs, vis)

    def live():
        return (jnp.max(c_ref[...]) > SB_EXIT).astype(jnp.int32)

    def cond(s):
        return jnp.logical_and(s[0] >= 0, s[1] > 0)

    def body(s):
        p = s[0]
        slot = lax.rem(p, 2)
        wait(p, slot)

        @pl.when(p >= 1)
        def _():
            start(p - 1, 1 - slot)

        _sb_chunk(qs,
                  [kbuf[slot, pl.ds(h, page, stride=n_heads), :].astype(BF16) for h in heads],
                  [vbuf[slot, pl.ds(h, page, stride=n_heads), :].astype(BF16) for h in heads],
                  u2, acc_refs, c_refs, None)
        return p - 1, live()

    p_end, _ = lax.while_loop(cond, body, (n_pages - 1, live()))

    @pl.when(p_end >= 0)
    def _():
        wait(p_end, lax.rem(p_end, 2))

    for h in range(n_heads):
        o_ref[:, h * dh:(h + 1) * dh] = acc_ref[h]


def _sb_decode(page_table, q, knew, vnew, kcache, vcache, n_heads, dh, ts):
    nb, n_pages = page_table.shape
    width = n_heads * dh
    rows = q.shape[2]
    page_rows = kcache.shape[1]
    grid_spec = pltpu.PrefetchScalarGridSpec(
        num_scalar_prefetch=1,
        grid=(nb,),
        in_specs=[pl.BlockSpec((None, n_heads, rows, dh), lambda b, pt: (b, 0, 0, 0)),
                  pl.BlockSpec((None, SUBLANES, width), lambda b, pt: (b, 0, 0)),
                  pl.BlockSpec((None, SUBLANES, width), lambda b, pt: (b, 0, 0)),
                  pl.BlockSpec(memory_space=pl.ANY),
                  pl.BlockSpec(memory_space=pl.ANY)],
        out_specs=pl.BlockSpec((None, rows, width), lambda b, pt: (b, 0, 0)),
        scratch_shapes=[pltpu.VMEM((2, page_rows, dh), F32), pltpu.VMEM((2, page_rows, dh), F32),
                        pltpu.SemaphoreType.DMA((2, 2)),
                        pltpu.VMEM((n_heads, rows, dh), F32), pltpu.VMEM((n_heads, rows, LANES), F32)],
    )
    return pl.pallas_call(
        functools.partial(_sb_decode_kernel, n_heads=n_heads, dh=dh, ts=ts, n_pages=n_pages),
        out_shape=jax.ShapeDtypeStruct((nb, rows, width), F32),
        grid_spec=grid_spec,
        compiler_params=_params(("arbitrary",)),
        name="sb_decode",
    )(page_table.reshape(-1), q, knew, vnew, kcache, vcache)


def _merge_kernel(osb_ref, odf_ref, gsb_ref, gdf_ref, x_ref, wsb_ref, wdf_ref, wout_ref, g_ref, x1_ref, h_ref):
    y = (gsb_ref[...].astype(F32) * _dot(osb_ref[...], wsb_ref[...])
         + gdf_ref[...].astype(F32) * _dot(odf_ref[...], wdf_ref[...]))
    x1 = x_ref[...] + _dot(y.astype(BF16), wout_ref[...])
    x1_ref[...] = x1
    h_ref[...] = _rms(x1, g_ref[...]).astype(BF16)


def _merge(o_sb, o_df, gates, x, w_sb_o, w_diff_o, w_out, norm_g):
    m, d = x.shape
    w1 = o_sb.shape[1]
    tm = min(m, 256)
    const = lambda i: (0, 0)
    return pl.pallas_call(
        _merge_kernel,
        out_shape=[jax.ShapeDtypeStruct((m, d), F32), jax.ShapeDtypeStruct((m, d), BF16)],
        grid=(m // tm,),
        in_specs=[pl.BlockSpec((tm, w1), lambda i: (i, 0)),
                  pl.BlockSpec((tm, w1), lambda i: (i, 0)),
                  pl.BlockSpec((tm, d), lambda i: (i, 0)),
                  pl.BlockSpec((tm, d), lambda i: (i, 1)),
                  pl.BlockSpec((tm, d), lambda i: (i, 0)),
                  pl.BlockSpec((w1, d), const, pipeline_mode=pl.Buffered(1)),
                  pl.BlockSpec((w1, d), const, pipeline_mode=pl.Buffered(1)),
                  pl.BlockSpec((d, d), const, pipeline_mode=pl.Buffered(1)),
                  pl.BlockSpec((1, d), const)],
        out_specs=[pl.BlockSpec((tm, d), lambda i: (i, 0)), pl.BlockSpec((tm, d), lambda i: (i, 0))],
        compiler_params=_params(("parallel",)),
        name="merge",
    )(o_sb, o_df, gates, gates, x, w_sb_o, w_diff_o, w_out, norm_g.reshape(1, d))


def _mem_attn_kernel(q_ref, mk_ref, mv_ref, o_ref, *, n_heads, dh):
    interleaved = mk_ref.shape[1] == dh
    n_mem = mk_ref.shape[0] // n_heads if interleaved else mk_ref.shape[0]
    for h in range(n_heads):
        sl = slice(h * dh, (h + 1) * dh)
        if interleaved:
            mk = mk_ref[pl.ds(h, n_mem, stride=n_heads), :]
            mv = mv_ref[pl.ds(h, n_mem, stride=n_heads), :]
        else:
            mk, mv = mk_ref[:, sl], mv_ref[:, sl]
        z = _dot_nt(q_ref[:, sl], mk.astype(BF16))
        p = jnp.exp(z - jnp.max(z, axis=1, keepdims=True))
        o = _dot(p.astype(BF16), mv.astype(BF16)) / jnp.sum(p, axis=1, keepdims=True)
        o_ref[:, sl] = o.astype(o_ref.dtype)


def _mem_attn(q, mk, mv, n_heads, dh):
    nb, m, w = q.shape
    tm = min(m, 512)
    return pl.pallas_call(
        functools.partial(_mem_attn_kernel, n_heads=n_heads, dh=dh),
        out_shape=jax.ShapeDtypeStruct((nb, m, w), BF16),
        grid=(nb, m // tm),
        in_specs=[pl.BlockSpec((None, tm, w), lambda b, i: (b, i, 0)),
                  pl.BlockSpec((None,) + mk.shape[1:], lambda b, i: (b, 0, 0)),
                  pl.BlockSpec((None,) + mv.shape[1:], lambda b, i: (b, 0, 0))],
        out_specs=pl.BlockSpec((None, tm, w), lambda b, i: (b, i, 0)),
        compiler_params=_params(("parallel", "arbitrary")),
        name="mem_attn",
    )(q, mk, mv)


def _ffn_kernel(*refs, sample, ts, tail):
    if sample:
        h_ref, p1_ref, p2_ref = refs[:3]
        pos = 3
    else:
        h_ref, halo_ref = refs[:2]
        pos = 2
    (x_ref, wg_ref, wu_ref, wd_ref, cw_ref, cb_ref, gf_ref, y_ref, gout_ref, acc_ref) = refs[pos:]
    i = pl.program_id(0)
    f = pl.program_id(1)
    tm, tf = h_ref.shape[0], wg_ref.shape[1]

    @pl.when(f == 0)
    def _():
        acc_ref[...] = jnp.zeros_like(acc_ref)

    h = h_ref[...]
    g = _dot(h, wg_ref[...])
    u = _dot(h, wu_ref[...])
    g1 = pltpu.roll(g, 1, 0)
    g2 = pltpu.roll(g, 2, 0)
    if sample:
        gout_ref[...] = g
        t = lax.rem(lax.broadcasted_iota(jnp.int32, (tm, tf), 0), ts)
        g1 = jnp.where(t >= 1, g1, p1_ref[...])
        g2 = jnp.where(t >= 2, g2, p2_ref[...])
    else:
        gout_ref[...] = g[tm - tail:, :]
        gh = _dot(halo_ref[...], wg_ref[...]) * (i > 0).astype(F32)
        row = lax.broadcasted_iota(jnp.int32, (SUBLANES, tf), 0)
        top1 = jnp.where(row < 1, pltpu.roll(gh, 1, 0)[:SUBLANES], g1[:SUBLANES])
        top2 = jnp.where(row < 2, pltpu.roll(gh, 2, 0)[:SUBLANES], g2[:SUBLANES])
        g1 = jnp.concatenate([top1, g1[SUBLANES:]], axis=0)
        g2 = jnp.concatenate([top2, g2[SUBLANES:]], axis=0)
    cw = cw_ref[...]
    c = cb_ref[...] + cw[0:1] * g2 + cw[1:2] * g1 + cw[2:3] * g
    a = c / (1.0 + jnp.exp(-c)) * u
    acc_ref[...] += _dot(a.astype(BF16), wd_ref[...])

    @pl.when(f == pl.num_programs(1) - 1)
    def _():
        y_ref[...] = _rms(x_ref[...] + acc_ref[...], gf_ref[...])


def _ffn(h, x, wg, wu, wd, cw, cb, norm_f, *, prev=None, ts=1, tf=512):
    m, d = x.shape
    fp = wg.shape[1]
    sample = prev is not None
    tm = min(m, 512)
    tail = SUBLANES
    assert fp % tf == 0 and m % tm == 0
    row = lambda i, f: (i, 0)
    in_specs = [pl.BlockSpec((tm, d), row)]
    args = [h]
    if sample:
        in_specs += [pl.BlockSpec((tm, tf), lambda i, f: (i, f))] * 2
        args += list(prev)
        g_shape, g_spec = (m, fp), pl.BlockSpec((tm, tf), lambda i, f: (i, f))
    else:
        halo_blocks = tm // HALO
        in_specs.append(pl.BlockSpec((HALO, d), lambda i, f: (jnp.maximum(i * halo_blocks - 1, 0), 0)))
        args.append(h)
        g_shape, g_spec = (m // tm * tail, fp), pl.BlockSpec((tail, tf), lambda i, f: (i, f))
    in_specs += [pl.BlockSpec((tm, d), row),
                 pl.BlockSpec((d, tf), lambda i, f: (0, f)),
                 pl.BlockSpec((d, tf), lambda i, f: (0, f)),
                 pl.BlockSpec((tf, d), lambda i, f: (f, 0)),
                 pl.BlockSpec((CONV_W, tf), lambda i, f: (0, f)),
                 pl.BlockSpec((1, tf), lambda i, f: (0, f)),
                 pl.BlockSpec((1, d), lambda i, f: (0, 0))]
    args += [x, wg, wu, wd, cw, cb.reshape(1, fp), norm_f.reshape(1, d)]
    return pl.pallas_call(
        functools.partial(_ffn_kernel, sample=sample, ts=ts, tail=tail),
        out_shape=[jax.ShapeDtypeStruct((m, d), F32), jax.ShapeDtypeStruct(g_shape, F32)],
        grid=(m // tm, fp // tf),
        in_specs=in_specs,
        out_specs=[pl.BlockSpec((tm, d), row), g_spec],
        scratch_shapes=[pltpu.VMEM((tm, d), F32)],
        compiler_params=_params(("arbitrary", "arbitrary")),
        name="conv_ffn",
    )(*args)


def _t5_bias_by_distance(t5_bias, n):
    d = jnp.arange(n, dtype=jnp.int32)
    max_exact = NUM_BUCKETS // 2
    df = jnp.maximum(d, 1).astype(F32)
    large = max_exact + (jnp.log(df / max_exact) / math.log(MAX_DISTANCE / max_exact)
                         * (NUM_BUCKETS - max_exact)).astype(jnp.int32)
    large = jnp.minimum(large, NUM_BUCKETS - 1)
    bucket = jnp.where(d < max_exact, d, large)
    return t5_bias.astype(F32)[bucket].T


def _toeplitz_tile(bias_d, blk, off):
    n = 2 * blk
    j = jnp.arange(n, dtype=jnp.int32)
    dist = off - jnp.where(j < blk, j, j - n)
    e = jnp.where(dist >= 0, bias_d[:, jnp.clip(dist, 0, bias_d.shape[1] - 1)], NEG)
    skew = jnp.tile(e, (1, blk))[:, :blk * (n - 1)].reshape(-1, blk, n - 1)
    return skew[:, :, :blk]


def _bias_rows(bias_d, toks, first_dist, n_keys, n_valid):
    rows = []
    for tok in toks:
        top = first_dist + tok
        n_ok = min(n_valid, top + 1)
        ok = jnp.flip(bias_d[:, top - n_ok + 1:top + 1], axis=1)
        rows.append(jnp.pad(ok, ((0, 0), (0, n_keys - n_ok)), constant_values=NEG))
    return jnp.stack(rows, axis=1)


def _pad_rows(a, rows):
    pad = [(0, 0)] * a.ndim
    pad[-2] = (0, rows - a.shape[-2])
    return jnp.pad(a, pad)


def kernel(x_prompt, x_sample, mem_prompt, cache_sb_k, cache_sb_v, cache_diff_k, cache_diff_v, cache_mem_k, cache_mem_v, state_conv, page_table, norm_mix, w_in, w_sb_o, w_diff_o, w_out, diff_subln_g, lambda_q1, lambda_k1, lambda_q2, lambda_k2, t5_bias, norm_cross, norm_mem, w_mq, w_mk, w_mv, w_mo, norm_ffn, w_gate, w_up, conv_w, conv_b, w_down, norm_f):
    depth = w_in.shape[0]
    assert depth == 1 and x_prompt.shape[0] == 1
    _, t, d = x_prompt.shape
    nb, ts, _ = x_sample.shape
    n_pool, page, h_sb, dh = cache_sb_k.shape[1:]
    h_df = cache_diff_k.shape[3]
    dv = 2 * dh
    h_mem = cache_mem_k.shape[3]
    n_mem = mem_prompt.shape[1]
    d_ff = w_gate.shape[2]
    n_pages = page_table.shape[1]
    past = n_pages * page
    sb_w = h_sb * dh
    scale = dh ** -0.5
    lam_init = 0.8 - 0.6 * math.exp(-0.3 * 0)
    out_scale = 1.0 - lam_init
    assert 2 * ts <= QROWS and ts >= CONV_W - 1 and page == LANES and DF_BLK >= MAX_DISTANCE

    lam = (jnp.exp(jnp.sum(lambda_q1[0].astype(F32) * lambda_k1[0].astype(F32)))
           - jnp.exp(jnp.sum(lambda_q2[0].astype(F32) * lambda_k2[0].astype(F32))) + lam_init).reshape(1)

    bias_d = _t5_bias_by_distance(t5_bias, max(2 * DF_BLK, PAGES_PER_STEP * page + QROWS))
    far = bias_d[:, MAX_DISTANCE]
    tiles = jnp.stack([_toeplitz_tile(bias_d, DF_BLK, 0), _toeplitz_tile(bias_d, DF_BLK, DF_BLK)], axis=1)

    tf = 512
    fp = ((d_ff + tf - 1) // tf) * tf
    w_in_b = w_in[0].astype(BF16)
    w_sb_o_b, w_diff_o_b, w_out_b = w_sb_o[0].astype(BF16), w_diff_o[0].astype(BF16), w_out[0].astype(BF16)
    w_mq_b, w_mk_b, w_mv_b, w_mo_b = (w[0].astype(BF16) for w in (w_mq, w_mk, w_mv, w_mo))
    wg_b = jnp.concatenate([w_gate[0].astype(BF16), jnp.zeros((d, fp - d_ff), BF16)], axis=1)
    wu_b = jnp.concatenate([w_up[0].astype(BF16), jnp.zeros((d, fp - d_ff), BF16)], axis=1)
    wd_b = jnp.concatenate([w_down[0].astype(BF16), jnp.zeros((fp - d_ff, d), BF16)], axis=0)
    cw_p = jnp.pad(conv_w[0], ((0, 0), (0, fp - d_ff)))
    cb_p = jnp.pad(conv_b[0], ((0, fp - d_ff),))

    def project(x2d, attn_copies):
        h = _rmsnorm_bf16(x2d, norm_mix[0])
        kv = [F32, BF16] if attn_copies else [F32]
        ident = list(range(sb_w // LANES)) if attn_copies else None
        v_slots = [2 * hh + j for j in range(2) for hh in range(h_df)] if attn_copies else None
        c = 0
        q_sb = _linear(h, w_in_b, c, sb_w, [BF16], scale=scale)[0]; c += sb_w
        k_sb = _linear(h, w_in_b, c, sb_w, kv, slots=ident); c += sb_w
        v_sb = _linear(h, w_in_b, c, sb_w, kv, slots=ident); c += sb_w
        q_df = _linear(h, w_in_b, c, h_df * dv, [BF16], scale=scale * LOG2E if attn_copies else scale)[0]
        c += h_df * dv
        k_df = _linear(h, w_in_b, c, h_df * dv, kv, slots=ident); c += h_df * dv
        v_df = _linear(h, w_in_b, c, h_df * dv, [F32], transposed=attn_copies, slots=v_slots); c += h_df * dv
        gates = _linear(h, w_in_b, c, 2 * d, [BF16], sigmoid=True, tn=2048)[0]
        return q_sb, k_sb, v_sb, q_df, k_df, v_df, gates

    def tail(x2d, o_sb, o_df, gates, mk, mv, rows_per_mem, ffn_prev, ffn_ts):
        x1, h2 = _merge(o_sb, o_df, gates, x2d, w_sb_o_b, w_diff_o_b, w_out_b, norm_cross[0])
        qm, = _linear(h2, w_mq_b, 0, h_mem * dh, [BF16], scale=scale)
        nbm = mk.shape[0]
        qm = qm.reshape(nbm, rows_per_mem, h_mem * dh)
        if rows_per_mem < QROWS:
            om = _mem_attn(_pad_rows(qm, QROWS), mk, mv, h_mem, dh)[:, :rows_per_mem]
        else:
            om = _mem_attn(qm, mk, mv, h_mem, dh)
        om = om.reshape(nbm * rows_per_mem, h_mem * dh)
        x2, h3 = _linear(om, w_mo_b, 0, d, [F32], res=x1, norm_g=norm_ffn[0])
        return _ffn(h3, x2, wg_b, wu_b, wd_b, cw_p, cb_p, norm_f, prev=ffn_prev, ts=ffn_ts, tf=tf)

    xp = x_prompt.reshape(t, d)
    q_sb, (k_sb, k_sb_b), (v_sb, v_sb_b), q_df, (k_df, k_df_b), (v_df, v_df_t), gates = project(xp, True)
    o_sb = _sb_prompt(q_sb, k_sb_b, v_sb_b, h_sb, dh)
    tiles_t = (jnp.swapaxes(tiles, 2, 3) - far[:, None, None, None]) * LOG2E
    o_df = _df_prompt(q_df, k_df_b, v_df_t, tiles_t, lam, diff_subln_g[0], h_df, dh, out_scale)
    m_b = _rmsnorm_bf16(mem_prompt.reshape(n_mem, d), norm_mem[0])
    mk, = _linear(m_b, w_mk_b, 0, h_mem * dh, [F32])
    mv, = _linear(m_b, w_mv_b, 0, h_mem * dh, [F32])
    y_p, g_tail = tail(xp, o_sb, o_df, gates, mk[None], mv[None], t, None, 1)
    conv_p = g_tail[g_tail.shape[0] - (CONV_W - 1):, :d_ff]

    xs = x_sample.reshape(nb * ts, d)
    qs_sb, (ks_sb,), (vs_sb,), qs_df, (ks_df,), (vs_df,), gates_s = project(xs, False)
    q1 = _pad_rows(qs_sb.reshape(nb, ts, h_sb, dh).transpose(0, 2, 1, 3), QROWS)
    os_sb = _sb_decode(page_table, q1, _pad_rows(ks_sb.reshape(nb, ts, sb_w), SUBLANES),
                       _pad_rows(vs_sb.reshape(nb, ts, sb_w), SUBLANES),
                       cache_sb_k.reshape(n_pool, page * h_sb, dh), cache_sb_v.reshape(n_pool, page * h_sb, dh),
                       h_sb, dh, ts)[:, :ts].astype(BF16)
    q2 = qs_df.reshape(nb, ts, h_df, 2, dh).transpose(0, 2, 3, 1, 4)
    zq = jnp.zeros_like(q2[:, :, 0])
    qbd = jnp.concatenate([jnp.concatenate([q2[:, :, 0], zq], axis=-1),
                           jnp.concatenate([zq, q2[:, :, 1]], axis=-1)], axis=2)
    qbd = _pad_rows(qbd, QROWS)
    toks = list(range(ts)) * 2 + [0] * (QROWS - 2 * ts)
    n_last = PAGES_PER_STEP * page
    blast = _bias_rows(bias_d, toks, n_last, n_last, n_last)
    bnew = _bias_rows(bias_d, toks, 0, page, ts)
    kd2 = cache_diff_k.reshape(n_pool, page * h_df * 2, dh)
    vd2 = cache_diff_v.reshape(n_pool, page, h_df, 2, dh).transpose(0, 1, 3, 2, 4).reshape(n_pool, page * 2 * h_df, dh)
    os_df = _df_decode(page_table, far, lam, qbd, blast, bnew, diff_subln_g[0],
                       _pad_rows(ks_df.reshape(nb, ts, h_df * dv), SUBLANES),
                       _pad_rows(vs_df.reshape(nb, ts, h_df * dv), SUBLANES),
                       kd2, vd2, h_df, dh, ts, out_scale)[:, :ts].astype(BF16)
    st = jnp.pad(state_conv[0], ((0, 0), (0, 0), (0, fp - d_ff)))
    zero = jnp.zeros((nb, 1, fp), F32)
    p1 = jnp.concatenate([st[:, 1:2]] + [zero] * (ts - 1), axis=1).reshape(nb * ts, fp)
    p2 = jnp.concatenate([st[:, 0:1], st[:, 1:2]] + [zero] * (ts - 2), axis=1).reshape(nb * ts, fp)
    y_s, g_s = tail(xs, os_sb.reshape(nb * ts, sb_w), os_df.reshape(nb * ts, h_df * dv), gates_s,
                    cache_mem_k.reshape(nb, n_mem * h_mem, dh), cache_mem_v.reshape(nb, n_mem * h_mem, dh),
                    ts, (p1, p2), ts)
    gp = jnp.concatenate([state_conv[0], g_s[:, :d_ff].reshape(nb, ts, d_ff)], axis=1)
    conv_s = gp[:, ts:]

    return (y_p.reshape(1, t, d), y_s.reshape(nb, ts, d),
            k_sb.reshape(1, 1, t, h_sb, dh), v_sb.reshape(1, 1, t, h_sb, dh),
            k_df.reshape(1, 1, t, h_df, 2, dh),
            v_df.reshape(t, 2, h_df, dh).transpose(0, 2, 1, 3).reshape(1, 1, t, h_df, dv),
            mk.reshape(1, 1, n_mem, h_mem, dh), mv.reshape(1, 1, n_mem, h_mem, dh),
            conv_p.reshape(1, 1, CONV_W - 1, d_ff),
            ks_sb.reshape(1, nb, ts, h_sb, dh), vs_sb.reshape(1, nb, ts, h_sb, dh),
            ks_df.reshape(1, nb, ts, h_df, 2, dh), vs_df.reshape(1, nb, ts, h_df, dv),
            conv_s.reshape(1, nb, CONV_W - 1, d_ff))
```

```python
import functools
import math

import jax
import jax.numpy as jnp
from jax import lax
from jax.experimental import pallas as pl
from jax.experimental.pallas import tpu as pltpu

F32 = jnp.float32
BF16 = jnp.bfloat16
EPS = 1e-6
NUM_BUCKETS = 32
MAX_DISTANCE = 128
CONV_W = 3
LANES = 128
SUBLANES = 8
VMEM_LIMIT = 56 * 1024 * 1024
NEG = -1e30
LOG2E = 1.4426950408889634
SB_EXIT = -120.0
SB_TK = 128
SB_HEADS_PER_STEP = 4
DF_BLK = 256
DF_FAR_BLOCKS = 4
PAGES_PER_STEP = 16
QROWS = 16
HALO = 16


def _params(sem):
    return pltpu.CompilerParams(dimension_semantics=sem, vmem_limit_bytes=VMEM_LIMIT)


def _dot(a, b):
    return jnp.dot(a, b, preferred_element_type=F32)


def _dot_nt(a, b):
    return lax.dot_general(a, b, (((1,), (1,)), ((), ())), preferred_element_type=F32)


def _rms(x, g):
    return x * lax.rsqrt(jnp.mean(x * x, axis=-1, keepdims=True) + EPS) * g


def _rms_kernel(x_ref, g_ref, o_ref):
    o_ref[...] = _rms(x_ref[...], g_ref[...]).astype(o_ref.dtype)


def _rmsnorm_bf16(x, g):
    m, d = x.shape
    tm = min(m, 512)
    return pl.pallas_call(
        _rms_kernel,
        out_shape=jax.ShapeDtypeStruct((m, d), BF16),
        grid=(m // tm,),
        in_specs=[pl.BlockSpec((tm, d), lambda i: (i, 0)), pl.BlockSpec((1, d), lambda i: (0, 0))],
        out_specs=pl.BlockSpec((tm, d), lambda i: (i, 0)),
        compiler_params=_params(("parallel",)),
        name="rmsnorm",
    )(x, g.reshape(1, d))


def _linear_kernel(*refs, scale, sigmoid, has_res, has_norm, n_plain, transposed, slots):
    x_ref, w_ref = refs[0], refs[1]
    pos = 2
    res_ref = g_ref = None
    if has_res:
        res_ref = refs[pos]
        pos += 1
    if has_norm:
        g_ref = refs[pos]
        pos += 1
    outs = refs[pos:]
    y = _dot(x_ref[...], w_ref[...])
    if scale is not None:
        y = y * scale
    if sigmoid:
        y = 1.0 / (1.0 + jnp.exp(-y))
    if has_res:
        y = y + res_ref[...]
    for k, o in enumerate(outs[:n_plain]):
        if k == 0 and slots is not None:
            for s, cb in enumerate(slots):
                o[pl.ds(s, y.shape[0], stride=len(slots)), :] = y[:, cb * LANES:(cb + 1) * LANES].astype(o.dtype)
        else:
            o[...] = y.astype(o.dtype)
    if has_norm:
        outs[n_plain][...] = _rms(y, g_ref[...]).astype(BF16)
    if transposed:
        outs[-1][...] = y.T.astype(BF16)


def _linear(x, w, col0, ncols, out_dtypes, *, scale=None, sigmoid=False, res=None, norm_g=None, tn=1024,
            transposed=False, slots=None):
    m, k = x.shape
    tm = min(m, 512)
    tn = min(tn, ncols)
    if norm_g is not None or slots is not None:
        tn = ncols
    assert m % tm == 0 and ncols % tn == 0 and col0 % tn == 0
    off = col0 // tn
    in_specs = [pl.BlockSpec((tm, k), lambda j, i: (i, 0)),
                pl.BlockSpec((k, tn), lambda j, i: (0, j + off))]
    args = [x, w]
    if res is not None:
        in_specs.append(pl.BlockSpec((tm, tn), lambda j, i: (i, j)))
        args.append(res)
    if norm_g is not None:
        in_specs.append(pl.BlockSpec((1, tn), lambda j, i: (0, 0)))
        args.append(norm_g.reshape(1, ncols))
    dts = list(out_dtypes) + ([BF16] if norm_g is not None else [])
    out_shape = [jax.ShapeDtypeStruct((m, ncols), dt) for dt in dts]
    out_specs = [pl.BlockSpec((tm, tn), lambda j, i: (i, j)) for _ in dts]
    if slots is not None:
        assert len(slots) * LANES == ncols
        out_shape[0] = jax.ShapeDtypeStruct((m * len(slots), LANES), dts[0])
        out_specs[0] = pl.BlockSpec((tm * len(slots), LANES), lambda j, i: (i, 0))
    if transposed:
        out_shape.append(jax.ShapeDtypeStruct((ncols, m), BF16))
        out_specs.append(pl.BlockSpec((tn, tm), lambda j, i: (j, i)))
    out = pl.pallas_call(
        functools.partial(_linear_kernel, scale=scale, sigmoid=sigmoid, has_res=res is not None,
                          has_norm=norm_g is not None, n_plain=len(out_dtypes), transposed=transposed,
                          slots=slots),
        out_shape=out_shape,
        grid=(ncols // tn, m // tm),
        in_specs=in_specs,
        out_specs=out_specs,
        compiler_params=_params(("parallel", "arbitrary")),
        name="linear",
    )(*args)
    return out


def _sb_consts(tk):
    r = lax.broadcasted_iota(jnp.int32, (tk, tk + LANES), 0)
    c = lax.broadcasted_iota(jnp.int32, (tk, tk + LANES), 1)
    return jnp.where((r > c) | (c >= tk), 1.0, 0.0).astype(BF16)


def _sb_chunk(qs, ks, vs, u2, acc_refs, c_refs, vis):
    tk = ks[0].shape[0]
    zs = [_dot_nt(q, k) for q, k in zip(qs, ks)]
    lks = []
    for z in zs:
        lk = -(jnp.maximum(z, 0.0) + jnp.log(1.0 + jnp.exp(-jnp.abs(z))))
        lks.append(lk if vis is None else jnp.where(vis, lk, 0.0))
    css = []
    for lk in lks:
        hi = lk.astype(BF16)
        lo = (lk - hi.astype(F32)).astype(BF16)
        css.append(_dot(hi, u2) + _dot(lo, u2))
    probs = []
    for z, lk, cs, c_ref in zip(zs, lks, css, c_refs):
        c = c_ref[...]
        a = jnp.exp(z + lk + cs[:, :tk] + c)
        probs.append((a if vis is None else jnp.where(vis, a, 0.0)).astype(BF16))
        c_ref[...] = c + cs[:, tk:]
    for a, v, acc_ref in zip(probs, vs, acc_refs):
        acc_ref[...] += _dot(a, v)


def _sb_prompt_kernel(q_ref, k_ref, v_ref, o_ref, acc_ref, c_ref, *, tq, dh):
    tk = SB_TK
    i = pl.program_id(1)
    nd = tq // tk
    heads = q_ref.shape[1] // dh
    acc_ref[...] = jnp.zeros_like(acc_ref)
    c_ref[...] = jnp.zeros_like(c_ref)
    u2 = _sb_consts(tk)
    row = lax.broadcasted_iota(jnp.int32, (tq, tk), 0)
    col = lax.broadcasted_iota(jnp.int32, (tq, tk), 1)

    def chunk(start, vis):
        sls = [slice(h * dh, (h + 1) * dh) for h in range(heads)]
        _sb_chunk([q_ref[:, sl] for sl in sls], [k_ref[pl.ds(start, tk), sl] for sl in sls],
                  [v_ref[pl.ds(start, tk), sl] for sl in sls], u2,
                  [acc_ref.at[h] for h in range(heads)], [c_ref.at[h] for h in range(heads)], vis)

    for d in range(nd - 1, -1, -1):
        chunk(pl.multiple_of(i * tq + d * tk, tk), (col + d * tk) < row)

    def live():
        return (jnp.max(c_ref[...]) > SB_EXIT).astype(jnp.int32)

    def cond(s):
        return jnp.logical_and(s[0] >= 0, s[1] > 0)

    def body(s):
        chunk(pl.multiple_of(s[0] * tk, tk), None)
        return s[0] - 1, live()

    lax.while_loop(cond, body, (i * nd - 1, live()))
    for h in range(heads):
        o_ref[:, h * dh:(h + 1) * dh] = acc_ref[h].astype(o_ref.dtype)


def _sb_prompt(q, k, v, n_heads, dh):
    t = q.shape[0]
    tq = min(t, 256)
    hs = SB_HEADS_PER_STEP
    assert n_heads % hs == 0
    w = hs * dh
    return pl.pallas_call(
        functools.partial(_sb_prompt_kernel, tq=tq, dh=dh),
        out_shape=jax.ShapeDtypeStruct((t, n_heads * dh), BF16),
        grid=(n_heads // hs, t // tq),
        in_specs=[pl.BlockSpec((tq, w), lambda h, i: (i, h)),
                  pl.BlockSpec((t, w), lambda h, i: (0, h)),
                  pl.BlockSpec((t, w), lambda h, i: (0, h))],
        out_specs=pl.BlockSpec((tq, w), lambda h, i: (i, h)),
        scratch_shapes=[pltpu.VMEM((hs, tq, dh), F32), pltpu.VMEM((hs, tq, LANES), F32)],
        compiler_params=_params(("parallel", "arbitrary")),
        name="sb_prompt",
    )(q, k, v)


def _softmax_steps(zs, vs, stats):
    alphas, ps = [], []
    for z, (m_ref, l_ref, _) in zip(zs, stats):
        m_old = m_ref[...]
        m_new = jnp.maximum(m_old, jnp.max(z, axis=1, keepdims=True))
        alpha = jnp.exp(m_old - m_new)
        p = jnp.exp(z - m_new)
        l_ref[...] = alpha * l_ref[...] + jnp.sum(p, axis=1, keepdims=True)
        m_ref[...] = m_new
        alphas.append(alpha)
        ps.append(p.astype(BF16))
    for alpha, p, v, (_, _, acc_ref) in zip(alphas, ps, vs, stats):
        acc_ref[...] = alpha * acc_ref[...] + _dot(p, v)


def _df_prompt_kernel(lam_ref, q_ref, k_ref, vt_ref, tiles_ref, g_ref, o_ref,
                      m_ref, l_ref, acc_ref, z_ref, *, dh, out_scale):
    blk = DF_BLK
    i = pl.program_id(1)
    m_ref[...] = jnp.full_like(m_ref, NEG)
    l_ref[...] = jnp.zeros_like(l_ref)
    acc_ref[...] = jnp.zeros_like(acc_ref)
    q = q_ref[...]

    def scores(first_blk, nblk):
        start = pl.multiple_of(first_blk * blk, blk)
        ks = k_ref[pl.ds(start, nblk * blk), :]
        return [_dot_nt(ks[:, c * dh:(c + 1) * dh], q[:, c * dh:(c + 1) * dh]) for c in range(2)]

    def step(first_blk, bias_t, nblk=1, zts=None):
        start = pl.multiple_of(first_blk * blk, blk)
        vt = vt_ref[:, pl.ds(start, nblk * blk)]
        if zts is None:
            zts = scores(first_blk, nblk)
        if bias_t is not None:
            zts = [zt + bias_t for zt in zts]
        alphas, ps = [], []
        for c, zt in enumerate(zts):
            m_old = m_ref[c]
            m_new = jnp.maximum(m_old, jnp.max(zt, axis=0, keepdims=True))
            alpha = jnp.exp2(m_old - m_new)
            p = jnp.exp2(zt - m_new)
            l_ref[c] = alpha * l_ref[c] + jnp.sum(p, axis=0, keepdims=True)
            m_ref[c] = m_new
            alphas.append(alpha)
            ps.append(p.astype(BF16))
        for c in range(2):
            acc_ref[c] = alphas[c] * acc_ref[c] + _dot(vt, ps[c])

    n_far = jnp.maximum(i - 1, 0)
    n_grp = n_far // DF_FAR_BLOCKS

    @pl.when(n_grp > 0)
    def _():
        for c, zt in enumerate(scores(0, DF_FAR_BLOCKS)):
            z_ref[0, c] = zt

    def far_body(j, carry):
        slot = lax.rem(j, 2)
        nxt = scores(jnp.minimum(j + 1, n_grp - 1) * DF_FAR_BLOCKS, DF_FAR_BLOCKS)
        step(j * DF_FAR_BLOCKS, None, DF_FAR_BLOCKS, zts=[z_ref[slot, 0], z_ref[slot, 1]])
        for c, zt in enumerate(nxt):
            z_ref[1 - slot, c] = zt
        return carry

    lax.fori_loop(0, n_grp, far_body, 0)

    def rest_body(kb, carry):
        step(kb, None)
        return carry

    lax.fori_loop(n_grp * DF_FAR_BLOCKS, n_far, rest_body, 0)

    @pl.when(i >= 1)
    def _():
        step(i - 1, tiles_ref[...], 2)

    @pl.when(i == 0)
    def _():
        step(0, tiles_ref[blk:, :])
    o_t = acc_ref[0] / l_ref[0] - lam_ref[0] * (acc_ref[1] / l_ref[1])
    o_ref[...] = (_rms(o_t.T, g_ref[...]) * out_scale).astype(o_ref.dtype)


def _df_prompt(q, k, v_t, tiles_t, lam, g, n_heads, dh, out_scale):
    t = q.shape[0]
    blk = DF_BLK
    dv = 2 * dh
    assert t % blk == 0
    grid_spec = pltpu.PrefetchScalarGridSpec(
        num_scalar_prefetch=1,
        grid=(n_heads, t // blk),
        in_specs=[pl.BlockSpec((blk, dv), lambda h, i, *_: (i, h)),
                  pl.BlockSpec((t, dv), lambda h, i, *_: (0, h)),
                  pl.BlockSpec((dv, t), lambda h, i, *_: (h, 0)),
                  pl.BlockSpec((None, 2 * blk, blk), lambda h, i, *_: (h, 0, 0)),
                  pl.BlockSpec((1, dv), lambda h, i, *_: (0, 0))],
        out_specs=pl.BlockSpec((blk, dv), lambda h, i, *_: (i, h)),
        scratch_shapes=[pltpu.VMEM((2, 1, blk), F32), pltpu.VMEM((2, 1, blk), F32),
                        pltpu.VMEM((2, dv, blk), F32),
                        pltpu.VMEM((2, 2, DF_FAR_BLOCKS * blk, blk), F32)],
    )
    return pl.pallas_call(
        functools.partial(_df_prompt_kernel, dh=dh, out_scale=out_scale),
        out_shape=jax.ShapeDtypeStruct((t, n_heads * dv), BF16),
        grid_spec=grid_spec,
        compiler_params=_params(("parallel", "arbitrary")),
        name="df_prompt",
    )(lam, q, k, v_t, tiles_t, g.reshape(1, dv))


def _df_decode_kernel(pt_ref, far_ref, lam_ref, q_ref, blast_ref, bnew_ref, g_ref, knew_ref, vnew_ref, *rest,
                      n_heads, dh, ts, out_scale):
    pg = PAGES_PER_STEP
    k_refs, v_refs = rest[:pg], rest[pg:2 * pg]
    o_ref, kbf, vbf, m_ref, l_ref, acc_ref = rest[2 * pg:]
    dv = 2 * dh
    s = pl.program_id(1)
    last = s == pl.num_programs(1) - 1
    page = kbf.shape[0] // pg

    @pl.when(s == 0)
    def _():
        m_ref[...] = jnp.full_like(m_ref, NEG)
        l_ref[...] = jnp.zeros_like(l_ref)
        acc_ref[...] = jnp.zeros_like(acc_ref)

    for r in range(pg):
        rows_r = slice(r * page, (r + 1) * page)
        for o in range(2 * n_heads):
            kbf[rows_r, o * dh:(o + 1) * dh] = k_refs[r][pl.ds(o, page, stride=2 * n_heads), :].astype(BF16)
            j, h = divmod(o, n_heads)
            vbf[rows_r, (2 * h + j) * dh:(2 * h + j + 1) * dh] = (
                v_refs[r][pl.ds(o, page, stride=2 * n_heads), :].astype(BF16))
    heads = range(n_heads)
    stats = [(m_ref.at[h], l_ref.at[h], acc_ref.at[h]) for h in heads]
    zs = [_dot_nt(q_ref[h], kbf[:, h * dv:(h + 1) * dv]) + jnp.where(last, blast_ref[h], far_ref[h]) for h in heads]
    _softmax_steps(zs, [vbf[:, h * dv:(h + 1) * dv] for h in heads], stats)

    @pl.when(last)
    def _():
        pad = jnp.zeros((page - knew_ref.shape[0], dv), F32)
        kns = [jnp.concatenate([knew_ref[:, h * dv:(h + 1) * dv], pad], axis=0).astype(BF16) for h in heads]
        vns = [jnp.concatenate([vnew_ref[:, h * dv:(h + 1) * dv], pad], axis=0).astype(BF16) for h in heads]
        _softmax_steps([_dot_nt(q_ref[h], kns[h]) + bnew_ref[h] for h in heads], vns, stats)
        for h in heads:
            on = acc_ref[h] / l_ref[h]
            o = on - lam_ref[0] * pltpu.roll(on, on.shape[0] - ts, 0)
            o_ref[:, h * dv:(h + 1) * dv] = _rms(o, g_ref[...]) * out_scale


def _df_decode(page_table, far, lam, qbd, blast, bnew, g, knew, vnew, kcache, vcache, n_heads, dh, ts, out_scale):
    nb, n_pages = page_table.shape
    pg = PAGES_PER_STEP
    assert n_pages % pg == 0
    dv = 2 * dh
    width = n_heads * dv
    page = kcache.shape[1] * dh // width
    rows = qbd.shape[2]

    def page_spec(r):
        return pl.BlockSpec((None,) + kcache.shape[1:], lambda b, s, pt, *_: (pt[b * n_pages + s * pg + r], 0, 0))

    grid_spec = pltpu.PrefetchScalarGridSpec(
        num_scalar_prefetch=3,
        grid=(nb, n_pages // pg),
        in_specs=[pl.BlockSpec((None, n_heads, rows, dv), lambda b, s, *_: (b, 0, 0, 0)),
                  pl.BlockSpec((n_heads, rows, pg * page), lambda b, s, *_: (0, 0, 0)),
                  pl.BlockSpec((n_heads, rows, page), lambda b, s, *_: (0, 0, 0)),
                  pl.BlockSpec((1, dv), lambda b, s, *_: (0, 0)),
                  pl.BlockSpec((None, SUBLANES, width), lambda b, s, *_: (b, 0, 0)),
                  pl.BlockSpec((None, SUBLANES, width), lambda b, s, *_: (b, 0, 0))]
                 + [page_spec(r) for r in range(pg)] + [page_spec(r) for r in range(pg)],
        out_specs=pl.BlockSpec((None, rows, width), lambda b, s, *_: (b, 0, 0)),
        scratch_shapes=[pltpu.VMEM((pg * page, width), BF16), pltpu.VMEM((pg * page, width), BF16),
                        pltpu.VMEM((n_heads, rows, 1), F32), pltpu.VMEM((n_heads, rows, 1), F32),
                        pltpu.VMEM((n_heads, rows, dv), F32)],
    )
    return pl.pallas_call(
        functools.partial(_df_decode_kernel, n_heads=n_heads, dh=dh, ts=ts, out_scale=out_scale),
        out_shape=jax.ShapeDtypeStruct((nb, rows, width), F32),
        grid_spec=grid_spec,
        compiler_params=_params(("parallel", "arbitrary")),
        name="df_decode",
    )(page_table.reshape(-1), far, lam, qbd, blast, bnew, g.reshape(1, dv), knew, vnew,
      *([kcache] * pg), *([vcache] * pg))


def _sb_decode_kernel(pt_ref, q_ref, knew_ref, vnew_ref, kc_ref, vc_ref, o_ref, kbuf, vbuf, sem, acc_ref, c_ref,
                      *, n_heads, dh, ts, n_pages):
    b = pl.program_id(0)
    page = kbuf.shape[1] // n_heads
    rows = q_ref.shape[1]
    u2 = _sb_consts(page)

    def page_copies(p, slot):
        pid = pt_ref[b * n_pages + p]
        return (pltpu.make_async_copy(kc_ref.at[pid], kbuf.at[slot], sem.at[0, slot]),
                pltpu.make_async_copy(vc_ref.at[pid], vbuf.at[slot], sem.at[1, slot]))

    def start(p, slot):
        for cp in page_copies(p, slot):
            cp.start()

    def wait(p, slot):
        for cp in page_copies(p, slot):
            cp.wait()

    start(n_pages - 1, (n_pages - 1) % 2)

    acc_ref[...] = jnp.zeros_like(acc_ref)
    c_ref[...] = jnp.zeros_like(c_ref)
    row = lax.broadcasted_iota(jnp.int32, (rows, page), 0)
    col = lax.broadcasted_iota(jnp.int32, (rows, page), 1)
    vis = col < jnp.minimum(row, ts)
    pad = jnp.zeros((page - knew_ref.shape[0], dh), F32)
    heads = range(n_heads)
    qs = [q_ref[h] for h in heads]
    acc_refs = [acc_ref.at[h] for h in heads]
    c_refs = [c_ref.at[h] for h in heads]
    _sb_chunk(qs,
              [jnp.concatenate([knew_ref[:, h * dh:(h + 1) * dh], pad], axis=0).astype(BF16) for h in heads],
              [jnp.concatenate([vnew_ref[:, h * dh:(h + 1) * dh], pad], axis=0).astype(BF16) for h in heads],
              u2, acc_refs, c_refs, vis)

    def live():
        return (jnp.max(c_ref[...]) > SB_EXIT).astype(jnp.int32)

    def cond(s):
        return jnp.logical_and(s[0] >= 0, s[1] > 0)

    def body(s):
        p = s[0]
        slot = lax.rem(p, 2)
        wait(p, slot)

        @pl.when(p >= 1)
        def _():
            start(p - 1, 1 - slot)

        _sb_chunk(qs,
                  [kbuf[slot, pl.ds(h, page, stride=n_heads), :].astype(BF16) for h in heads],
                  [vbuf[slot, pl.ds(h, page, stride=n_heads), :].astype(BF16) for h in heads],
                  u2, acc_refs, c_refs, None)
        return p - 1, live()

    p_end, _ = lax.while_loop(cond, body, (n_pages - 1, live()))

    @pl.when(p_end >= 0)
    def _():
        wait(p_end, lax.rem(p_end, 2))

    for h in range(n_heads):
        o_ref[:, h * dh:(h + 1) * dh] = acc_ref[h]


def _sb_decode(page_table, q, knew, vnew, kcache, vcache, n_heads, dh, ts):
    nb, n_pages = page_table.shape
    width = n_heads * dh
    rows = q.shape[2]
    page_rows = kcache.shape[1]
    grid_spec = pltpu.PrefetchScalarGridSpec(
        num_scalar_prefetch=1,
        grid=(nb,),
        in_specs=[pl.BlockSpec((None, n_heads, rows, dh), lambda b, pt: (b, 0, 0, 0)),
                  pl.BlockSpec((None, SUBLANES, width), lambda b, pt: (b, 0, 0)),
                  pl.BlockSpec((None, SUBLANES, width), lambda b, pt: (b, 0, 0)),
                  pl.BlockSpec(memory_space=pl.ANY),
                  pl.BlockSpec(memory_space=pl.ANY)],
        out_specs=pl.BlockSpec((None, rows, width), lambda b, pt: (b, 0, 0)),
        scratch_shapes=[pltpu.VMEM((2, page_rows, dh), F32), pltpu.VMEM((2, page_rows, dh), F32),
                        pltpu.SemaphoreType.DMA((2, 2)),
                        pltpu.VMEM((n_heads, rows, dh), F32), pltpu.VMEM((n_heads, rows, LANES), F32)],
    )
    return pl.pallas_call(
        functools.partial(_sb_decode_kernel, n_heads=n_heads, dh=dh, ts=ts, n_pages=n_pages),
        out_shape=jax.ShapeDtypeStruct((nb, rows, width), F32),
        grid_spec=grid_spec,
        compiler_params=_params(("arbitrary",)),
        name="sb_decode",
    )(page_table.reshape(-1), q, knew, vnew, kcache, vcache)


def _merge_kernel(osb_ref, odf_ref, gsb_ref, gdf_ref, x_ref, wsb_ref, wdf_ref, wout_ref, g_ref, x1_ref, h_ref):
    y = (gsb_ref[...].astype(F32) * _dot(osb_ref[...], wsb_ref[...])
         + gdf_ref[...].astype(F32) * _dot(odf_ref[...], wdf_ref[...]))
    x1 = x_ref[...] + _dot(y.astype(BF16), wout_ref[...])
    x1_ref[...] = x1
    h_ref[...] = _rms(x1, g_ref[...]).astype(BF16)


def _merge(o_sb, o_df, gates, x, w_sb_o, w_diff_o, w_out, norm_g):
    m, d = x.shape
    w1 = o_sb.shape[1]
    tm = min(m, 256)
    const = lambda i: (0, 0)
    return pl.pallas_call(
        _merge_kernel,
        out_shape=[jax.ShapeDtypeStruct((m, d), F32), jax.ShapeDtypeStruct((m, d), BF16)],
        grid=(m // tm,),
        in_specs=[pl.BlockSpec((tm, w1), lambda i: (i, 0)),
                  pl.BlockSpec((tm, w1), lambda i: (i, 0)),
                  pl.BlockSpec((tm, d), lambda i: (i, 0)),
                  pl.BlockSpec((tm, d), lambda i: (i, 1)),
                  pl.BlockSpec((tm, d), lambda i: (i, 0)),
                  pl.BlockSpec((w1, d), const, pipeline_mode=pl.Buffered(1)),
                  pl.BlockSpec((w1, d), const, pipeline_mode=pl.Buffered(1)),
                  pl.BlockSpec((d, d), const, pipeline_mode=pl.Buffered(1)),
                  pl.BlockSpec((1, d), const)],
        out_specs=[pl.BlockSpec((tm, d), lambda i: (i, 0)), pl.BlockSpec((tm, d), lambda i: (i, 0))],
        compiler_params=_params(("parallel",)),
        name="merge",
    )(o_sb, o_df, gates, gates, x, w_sb_o, w_diff_o, w_out, norm_g.reshape(1, d))


def _mem_attn_kernel(q_ref, mk_ref, mv_ref, o_ref, *, n_heads, dh):
    interleaved = mk_ref.shape[1] == dh
    n_mem = mk_ref.shape[0] // n_heads if interleaved else mk_ref.shape[0]
    for h in range(n_heads):
        sl = slice(h * dh, (h + 1) * dh)
        if interleaved:
            mk = mk_ref[pl.ds(h, n_mem, stride=n_heads), :]
            mv = mv_ref[pl.ds(h, n_mem, stride=n_heads), :]
        else:
            mk, mv = mk_ref[:, sl], mv_ref[:, sl]
        z = _dot_nt(q_ref[:, sl], mk.astype(BF16))
        p = jnp.exp(z - jnp.max(z, axis=1, keepdims=True))
        o = _dot(p.astype(BF16), mv.astype(BF16)) / jnp.sum(p, axis=1, keepdims=True)
        o_ref[:, sl] = o.astype(o_ref.dtype)


def _mem_attn(q, mk, mv, n_heads, dh):
    nb, m, w = q.shape
    tm = min(m, 512)
    return pl.pallas_call(
        functools.partial(_mem_attn_kernel, n_heads=n_heads, dh=dh),
        out_shape=jax.ShapeDtypeStruct((nb, m, w), BF16),
        grid=(nb, m // tm),
        in_specs=[pl.BlockSpec((None, tm, w), lambda b, i: (b, i, 0)),
                  pl.BlockSpec((None,) + mk.shape[1:], lambda b, i: (b, 0, 0)),
                  pl.BlockSpec((None,) + mv.shape[1:], lambda b, i: (b, 0, 0))],
        out_specs=pl.BlockSpec((None, tm, w), lambda b, i: (b, i, 0)),
        compiler_params=_params(("parallel", "arbitrary")),
        name="mem_attn",
    )(q, mk, mv)


def _ffn_kernel(*refs, sample, ts, tail):
    if sample:
        h_ref, p1_ref, p2_ref = refs[:3]
        pos = 3
    else:
        h_ref, halo_ref = refs[:2]
        pos = 2
    (x_ref, wg_ref, wu_ref, wd_ref, cw_ref, cb_ref, gf_ref, y_ref, gout_ref, acc_ref) = refs[pos:]
    i = pl.program_id(0)
    f = pl.program_id(1)
    tm, tf = h_ref.shape[0], wg_ref.shape[1]

    @pl.when(f == 0)
    def _():
        acc_ref[...] = jnp.zeros_like(acc_ref)

    h = h_ref[...]
    g = _dot(h, wg_ref[...])
    u = _dot(h, wu_ref[...])
    g1 = pltpu.roll(g, 1, 0)
    g2 = pltpu.roll(g, 2, 0)
    if sample:
        gout_ref[...] = g
        t = lax.rem(lax.broadcasted_iota(jnp.int32, (tm, tf), 0), ts)
        g1 = jnp.where(t >= 1, g1, p1_ref[...])
        g2 = jnp.where(t >= 2, g2, p2_ref[...])
    else:
        gout_ref[...] = g[tm - tail:, :]
        gh = _dot(halo_ref[...], wg_ref[...]) * (i > 0).astype(F32)
        row = lax.broadcasted_iota(jnp.int32, (SUBLANES, tf), 0)
        top1 = jnp.where(row < 1, pltpu.roll(gh, 1, 0)[:SUBLANES], g1[:SUBLANES])
        top2 = jnp.where(row < 2, pltpu.roll(gh, 2, 0)[:SUBLANES], g2[:SUBLANES])
        g1 = jnp.concatenate([top1, g1[SUBLANES:]], axis=0)
        g2 = jnp.concatenate([top2, g2[SUBLANES:]], axis=0)
    cw = cw_ref[...]
    c = cb_ref[...] + cw[0:1] * g2 + cw[1:2] * g1 + cw[2:3] * g
    a = c / (1.0 + jnp.exp(-c)) * u
    acc_ref[...] += _dot(a.astype(BF16), wd_ref[...])

    @pl.when(f == pl.num_programs(1) - 1)
    def _():
        y_ref[...] = _rms(x_ref[...] + acc_ref[...], gf_ref[...])


def _ffn(h, x, wg, wu, wd, cw, cb, norm_f, *, prev=None, ts=1, tf=512):
    m, d = x.shape
    fp = wg.shape[1]
    sample = prev is not None
    tm = min(m, 512)
    tail = SUBLANES
    assert fp % tf == 0 and m % tm == 0
    row = lambda i, f: (i, 0)
    in_specs = [pl.BlockSpec((tm, d), row)]
    args = [h]
    if sample:
        in_specs += [pl.BlockSpec((tm, tf), lambda i, f: (i, f))] * 2
        args += list(prev)
        g_shape, g_spec = (m, fp), pl.BlockSpec((tm, tf), lambda i, f: (i, f))
    else:
        halo_blocks = tm // HALO
        in_specs.append(pl.BlockSpec((HALO, d), lambda i, f: (jnp.maximum(i * halo_blocks - 1, 0), 0)))
        args.append(h)
        g_shape, g_spec = (m // tm * tail, fp), pl.BlockSpec((tail, tf), lambda i, f: (i, f))
    in_specs += [pl.BlockSpec((tm, d), row),
                 pl.BlockSpec((d, tf), lambda i, f: (0, f)),
                 pl.BlockSpec((d, tf), lambda i, f: (0, f)),
                 pl.BlockSpec((tf, d), lambda i, f: (f, 0)),
                 pl.BlockSpec((CONV_W, tf), lambda i, f: (0, f)),
                 pl.BlockSpec((1, tf), lambda i, f: (0, f)),
                 pl.BlockSpec((1, d), lambda i, f: (0, 0))]
    args += [x, wg, wu, wd, cw, cb.reshape(1, fp), norm_f.reshape(1, d)]
    return pl.pallas_call(
        functools.partial(_ffn_kernel, sample=sample, ts=ts, tail=tail),
        out_shape=[jax.ShapeDtypeStruct((m, d), F32), jax.ShapeDtypeStruct(g_shape, F32)],
        grid=(m // tm, fp // tf),
        in_specs=in_specs,
        out_specs=[pl.BlockSpec((tm, d), row), g_spec],
        scratch_shapes=[pltpu.VMEM((tm, d), F32)],
        compiler_params=_params(("arbitrary", "arbitrary")),
        name="conv_ffn",
    )(*args)


def _t5_bias_by_distance(t5_bias, n):
    d = jnp.arange(n, dtype=jnp.int32)
    max_exact = NUM_BUCKETS // 2
    df = jnp.maximum(d, 1).astype(F32)
    large = max_exact + (jnp.log(df / max_exact) / math.log(MAX_DISTANCE / max_exact)
                         * (NUM_BUCKETS - max_exact)).astype(jnp.int32)
    large = jnp.minimum(large, NUM_BUCKETS - 1)
    bucket = jnp.where(d < max_exact, d, large)
    return t5_bias.astype(F32)[bucket].T


def _toeplitz_tile(bias_d, blk, off):
    n = 2 * blk
    j = jnp.arange(n, dtype=jnp.int32)
    dist = off - jnp.where(j < blk, j, j - n)
    e = jnp.where(dist >= 0, bias_d[:, jnp.clip(dist, 0, bias_d.shape[1] - 1)], NEG)
    skew = jnp.tile(e, (1, blk))[:, :blk * (n - 1)].reshape(-1, blk, n - 1)
    return skew[:, :, :blk]


def _bias_rows(bias_d, toks, first_dist, n_keys, n_valid):
    rows = []
    for tok in toks:
        top = first_dist + tok
        n_ok = min(n_valid, top + 1)
        ok = jnp.flip(bias_d[:, top - n_ok + 1:top + 1], axis=1)
        rows.append(jnp.pad(ok, ((0, 0), (0, n_keys - n_ok)), constant_values=NEG))
    return jnp.stack(rows, axis=1)


def _pad_rows(a, rows):
    pad = [(0, 0)] * a.ndim
    pad[-2] = (0, rows - a.shape[-2])
    return jnp.pad(a, pad)


def kernel(x_prompt, x_sample, mem_prompt, cache_sb_k, cache_sb_v, cache_diff_k, cache_diff_v, cache_mem_k, cache_mem_v, state_conv, page_table, norm_mix, w_in, w_sb_o, w_diff_o, w_out, diff_subln_g, lambda_q1, lambda_k1, lambda_q2, lambda_k2, t5_bias, norm_cross, norm_mem, w_mq, w_mk, w_mv, w_mo, norm_ffn, w_gate, w_up, conv_w, conv_b, w_down, norm_f):
    depth = w_in.shape[0]
    assert depth == 1 and x_prompt.shape[0] == 1
    _, t, d = x_prompt.shape
    nb, ts, _ = x_sample.shape
    n_pool, page, h_sb, dh = cache_sb_k.shape[1:]
    h_df = cache_diff_k.shape[3]
    dv = 2 * dh
    h_mem = cache_mem_k.shape[3]
    n_mem = mem_prompt.shape[1]
    d_ff = w_gate.shape[2]
    n_pages = page_table.shape[1]
    past = n_pages * page
    sb_w = h_sb * dh
    scale = dh ** -0.5
    lam_init = 0.8 - 0.6 * math.exp(-0.3 * 0)
    out_scale = 1.0 - lam_init
    assert 2 * ts <= QROWS and ts >= CONV_W - 1 and page == LANES and DF_BLK >= MAX_DISTANCE

    lam = (jnp.exp(jnp.sum(lambda_q1[0].astype(F32) * lambda_k1[0].astype(F32)))
           - jnp.exp(jnp.sum(lambda_q2[0].astype(F32) * lambda_k2[0].astype(F32))) + lam_init).reshape(1)

    bias_d = _t5_bias_by_distance(t5_bias, max(2 * DF_BLK, PAGES_PER_STEP * page + QROWS))
    far = bias_d[:, MAX_DISTANCE]
    tiles = jnp.concatenate([jnp.swapaxes(_toeplitz_tile(bias_d, DF_BLK, DF_BLK), 1, 2),
                             jnp.swapaxes(_toeplitz_tile(bias_d, DF_BLK, 0), 1, 2)], axis=1)

    tf = 512
    fp = ((d_ff + tf - 1) // tf) * tf
    w_in_b = w_in[0].astype(BF16)
    w_sb_o_b, w_diff_o_b, w_out_b = w_sb_o[0].astype(BF16), w_diff_o[0].astype(BF16), w_out[0].astype(BF16)
    w_mq_b, w_mk_b, w_mv_b, w_mo_b = (w[0].astype(BF16) for w in (w_mq, w_mk, w_mv, w_mo))
    wg_b = jnp.concatenate([w_gate[0].astype(BF16), jnp.zeros((d, fp - d_ff), BF16)], axis=1)
    wu_b = jnp.concatenate([w_up[0].astype(BF16), jnp.zeros((d, fp - d_ff), BF16)], axis=1)
    wd_b = jnp.concatenate([w_down[0].astype(BF16), jnp.zeros((fp - d_ff, d), BF16)], axis=0)
    cw_p = jnp.pad(conv_w[0], ((0, 0), (0, fp - d_ff)))
    cb_p = jnp.pad(conv_b[0], ((0, fp - d_ff),))

    def project(x2d, attn_copies):
        h = _rmsnorm_bf16(x2d, norm_mix[0])
        kv = [F32, BF16] if attn_copies else [F32]
        ident = list(range(sb_w // LANES)) if attn_copies else None
        v_slots = [2 * hh + j for j in range(2) for hh in range(h_df)] if attn_copies else None
        c = 0
        q_sb = _linear(h, w_in_b, c, sb_w, [BF16], scale=scale)[0]; c += sb_w
        k_sb = _linear(h, w_in_b, c, sb_w, kv, slots=ident); c += sb_w
        v_sb = _linear(h, w_in_b, c, sb_w, kv, slots=ident); c += sb_w
        q_df = _linear(h, w_in_b, c, h_df * dv, [BF16], scale=scale * LOG2E if attn_copies else scale)[0]
        c += h_df * dv
        k_df = _linear(h, w_in_b, c, h_df * dv, kv, slots=ident); c += h_df * dv
        v_df = _linear(h, w_in_b, c, h_df * dv, [F32], transposed=attn_copies, slots=v_slots); c += h_df * dv
        gates = _linear(h, w_in_b, c, 2 * d, [BF16], sigmoid=True, tn=2048)[0]
        return q_sb, k_sb, v_sb, q_df, k_df, v_df, gates

    def tail(x2d, o_sb, o_df, gates, mk, mv, rows_per_mem, ffn_prev, ffn_ts):
        x1, h2 = _merge(o_sb, o_df, gates, x2d, w_sb_o_b, w_diff_o_b, w_out_b, norm_cross[0])
        qm, = _linear(h2, w_mq_b, 0, h_mem * dh, [BF16], scale=scale)
        nbm = mk.shape[0]
        qm = qm.reshape(nbm, rows_per_mem, h_mem * dh)
        if rows_per_mem < QROWS:
            om = _mem_attn(_pad_rows(qm, QROWS), mk, mv, h_mem, dh)[:, :rows_per_mem]
        else:
            om = _mem_attn(qm, mk, mv, h_mem, dh)
        om = om.reshape(nbm * rows_per_mem, h_mem * dh)
        x2, h3 = _linear(om, w_mo_b, 0, d, [F32], res=x1, norm_g=norm_ffn[0])
        return _ffn(h3, x2, wg_b, wu_b, wd_b, cw_p, cb_p, norm_f, prev=ffn_prev, ts=ffn_ts, tf=tf)

    xp = x_prompt.reshape(t, d)
    q_sb, (k_sb, k_sb_b), (v_sb, v_sb_b), q_df, (k_df, k_df_b), (v_df, v_df_t), gates = project(xp, True)
    o_sb = _sb_prompt(q_sb, k_sb_b, v_sb_b, h_sb, dh)
    tiles_t = (tiles - far[:, None, None]) * LOG2E
    o_df = _df_prompt(q_df, k_df_b, v_df_t, tiles_t, lam, diff_subln_g[0], h_df, dh, out_scale)
    m_b = _rmsnorm_bf16(mem_prompt.reshape(n_mem, d), norm_mem[0])
    mk, = _linear(m_b, w_mk_b, 0, h_mem * dh, [F32])
    mv, = _linear(m_b, w_mv_b, 0, h_mem * dh, [F32])
    y_p, g_tail = tail(xp, o_sb, o_df, gates, mk[None], mv[None], t, None, 1)
    conv_p = g_tail[g_tail.shape[0] - (CONV_W - 1):, :d_ff]

    xs = x_sample.reshape(nb * ts, d)
    qs_sb, (ks_sb,), (vs_sb,), qs_df, (ks_df,), (vs_df,), gates_s = project(xs, False)
    q1 = _pad_rows(qs_sb.reshape(nb, ts, h_sb, dh).transpose(0, 2, 1, 3), QROWS)
    os_sb = _sb_decode(page_table, q1, _pad_rows(ks_sb.reshape(nb, ts, sb_w), SUBLANES),
                       _pad_rows(vs_sb.reshape(nb, ts, sb_w), SUBLANES),
                       cache_sb_k.reshape(n_pool, page * h_sb, dh), cache_sb_v.reshape(n_pool, page * h_sb, dh),
                       h_sb, dh, ts)[:, :ts].astype(BF16)
    q2 = qs_df.reshape(nb, ts, h_df, 2, dh).transpose(0, 2, 3, 1, 4)
    zq = jnp.zeros_like(q2[:, :, 0])
    qbd = jnp.concatenate([jnp.concatenate([q2[:, :, 0], zq], axis=-1),
                           jnp.concatenate([zq, q2[:, :, 1]], axis=-1)], axis=2)
    qbd = _pad_rows(qbd, QROWS)
    toks = list(range(ts)) * 2 + [0] * (QROWS - 2 * ts)
    n_last = PAGES_PER_STEP * page
    blast = _bias_rows(bias_d, toks, n_last, n_last, n_last)
    bnew = _bias_rows(bias_d, toks, 0, page, ts)
    kd2 = cache_diff_k.reshape(n_pool, page * h_df * 2, dh)
    vd2 = cache_diff_v.reshape(n_pool, page, h_df, 2, dh).transpose(0, 1, 3, 2, 4).reshape(n_pool, page * 2 * h_df, dh)
    os_df = _df_decode(page_table, far, lam, qbd, blast, bnew, diff_subln_g[0],
                       _pad_rows(ks_df.reshape(nb, ts, h_df * dv), SUBLANES),
                       _pad_rows(vs_df.reshape(nb, ts, h_df * dv), SUBLANES),
                       kd2, vd2, h_df, dh, ts, out_scale)[:, :ts].astype(BF16)
    st = jnp.pad(state_conv[0], ((0, 0), (0, 0), (0, fp - d_ff)))
    zero = jnp.zeros((nb, 1, fp), F32)
    p1 = jnp.concatenate([st[:, 1:2]] + [zero] * (ts - 1), axis=1).reshape(nb * ts, fp)
    p2 = jnp.concatenate([st[:, 0:1], st[:, 1:2]] + [zero] * (ts - 2), axis=1).reshape(nb * ts, fp)
    y_s, g_s = tail(xs, os_sb.reshape(nb * ts, sb_w), os_df.reshape(nb * ts, h_df * dv), gates_s,
                    cache_mem_k.reshape(nb, n_mem * h_mem, dh), cache_mem_v.reshape(nb, n_mem * h_mem, dh),
                    ts, (p1, p2), ts)
    gp = jnp.concatenate([state_conv[0], g_s[:, :d_ff].reshape(nb, ts, d_ff)], axis=1)
    conv_s = gp[:, ts:]

    return (y_p.reshape(1, t, d), y_s.reshape(nb, ts, d),
            k_sb.reshape(1, 1, t, h_sb, dh), v_sb.reshape(1, 1, t, h_sb, dh),
            k_df.reshape(1, 1, t, h_df, 2, dh),
            v_df.reshape(t, 2, h_df, dh).transpose(0, 2, 1, 3).reshape(1, 1, t, h_df, dv),
            mk.reshape(1, 1, n_mem, h_mem, dh), mv.reshape(1, 1, n_mem, h_mem, dh),
            conv_p.reshape(1, 1, CONV_W - 1, d_ff),
            ks_sb.reshape(1, nb, ts, h_sb, dh), vs_sb.reshape(1, nb, ts, h_sb, dh),
            ks_df.reshape(1, nb, ts, h_df, 2, dh), vs_df.reshape(1, nb, ts, h_df, dv),
            conv_s.reshape(1, nb, CONV_W - 1, d_ff))
```

```python
import functools
import math

import jax
import jax.numpy as jnp
from jax import lax
from jax.experimental import pallas as pl
from jax.experimental.pallas import tpu as pltpu

F32 = jnp.float32
BF16 = jnp.bfloat16
EPS = 1e-6
NUM_BUCKETS = 32
MAX_DISTANCE = 128
CONV_W = 3
LANES = 128
SUBLANES = 8
VMEM_LIMIT = 56 * 1024 * 1024
NEG = -1e30
LOG2E = 1.4426950408889634
SB_EXIT = -120.0
SB_TK = 128
SB_HEADS_PER_STEP = 4
DF_BLK = 256
DF_FAR_BLOCKS = 4
PAGES_PER_STEP = 8
PAGE_BUFFERS = 3
QROWS = 16
HALO = 16


def _params(sem):
    return pltpu.CompilerParams(dimension_semantics=sem, vmem_limit_bytes=VMEM_LIMIT)


def _dot(a, b):
    return jnp.dot(a, b, preferred_element_type=F32)


def _dot_nt(a, b):
    return lax.dot_general(a, b, (((1,), (1,)), ((), ())), preferred_element_type=F32)


def _rms(x, g):
    return x * lax.rsqrt(jnp.mean(x * x, axis=-1, keepdims=True) + EPS) * g


def _rms_kernel(x_ref, g_ref, o_ref):
    o_ref[...] = _rms(x_ref[...], g_ref[...]).astype(o_ref.dtype)


def _rmsnorm_bf16(x, g):
    m, d = x.shape
    tm = min(m, 512)
    return pl.pallas_call(
        _rms_kernel,
        out_shape=jax.ShapeDtypeStruct((m, d), BF16),
        grid=(m // tm,),
        in_specs=[pl.BlockSpec((tm, d), lambda i: (i, 0)), pl.BlockSpec((1, d), lambda i: (0, 0))],
        out_specs=pl.BlockSpec((tm, d), lambda i: (i, 0)),
        compiler_params=_params(("parallel",)),
        name="rmsnorm",
    )(x, g.reshape(1, d))


def _cast_pad_kernel(w_ref, o_ref, *, n_rows, n_cols):
    tr, tc = o_ref.shape
    r = pl.program_id(0) * tr + lax.broadcasted_iota(jnp.int32, (tr, tc), 0)
    c = pl.program_id(1) * tc + lax.broadcasted_iota(jnp.int32, (tr, tc), 1)
    w = jnp.where(r < n_rows, jnp.where(c < n_cols, w_ref[...], 0.0), 0.0)
    o_ref[...] = w.astype(o_ref.dtype)


def _cast_pad_bf16(w, rows, cols, tile=512):
    assert rows % tile == 0 and cols % tile == 0
    return pl.pallas_call(
        functools.partial(_cast_pad_kernel, n_rows=w.shape[0], n_cols=w.shape[1]),
        out_shape=jax.ShapeDtypeStruct((rows, cols), BF16),
        grid=(rows // tile, cols // tile),
        in_specs=[pl.BlockSpec((tile, tile), lambda i, j: (i, j))],
        out_specs=pl.BlockSpec((tile, tile), lambda i, j: (i, j)),
        compiler_params=_params(("parallel", "parallel")),
        name="cast_pad",
    )(w)


def _linear_kernel(*refs, scale, sigmoid, has_res, has_norm, n_plain, transposed, slots):
    x_ref, w_ref = refs[0], refs[1]
    pos = 2
    res_ref = g_ref = None
    if has_res:
        res_ref = refs[pos]
        pos += 1
    if has_norm:
        g_ref = refs[pos]
        pos += 1
    outs = refs[pos:]
    y = _dot(x_ref[...], w_ref[...])
    if scale is not None:
        y = y * scale
    if sigmoid:
        y = 1.0 / (1.0 + jnp.exp(-y))
    if has_res:
        y = y + res_ref[...]
    for k, o in enumerate(outs[:n_plain]):
        if k == 0 and slots is not None:
            for s, cb in enumerate(slots):
                o[pl.ds(s, y.shape[0], stride=len(slots)), :] = y[:, cb * LANES:(cb + 1) * LANES].astype(o.dtype)
        else:
            o[...] = y.astype(o.dtype)
    if has_norm:
        outs[n_plain][...] = _rms(y, g_ref[...]).astype(BF16)
    if transposed:
        outs[-1][...] = y.T.astype(BF16)


def _linear(x, w, col0, ncols, out_dtypes, *, scale=None, sigmoid=False, res=None, norm_g=None, tn=1024,
            transposed=False, slots=None):
    m, k = x.shape
    tm = min(m, 512)
    tn = min(tn, ncols)
    if norm_g is not None or slots is not None:
        tn = ncols
    assert m % tm == 0 and ncols % tn == 0 and col0 % tn == 0
    off = col0 // tn
    in_specs = [pl.BlockSpec((tm, k), lambda j, i: (i, 0)),
                pl.BlockSpec((k, tn), lambda j, i: (0, j + off))]
    args = [x, w]
    if res is not None:
        in_specs.append(pl.BlockSpec((tm, tn), lambda j, i: (i, j)))
        args.append(res)
    if norm_g is not None:
        in_specs.append(pl.BlockSpec((1, tn), lambda j, i: (0, 0)))
        args.append(norm_g.reshape(1, ncols))
    dts = list(out_dtypes) + ([BF16] if norm_g is not None else [])
    out_shape = [jax.ShapeDtypeStruct((m, ncols), dt) for dt in dts]
    out_specs = [pl.BlockSpec((tm, tn), lambda j, i: (i, j)) for _ in dts]
    if slots is not None:
        assert len(slots) * LANES == ncols
        out_shape[0] = jax.ShapeDtypeStruct((m * len(slots), LANES), dts[0])
        out_specs[0] = pl.BlockSpec((tm * len(slots), LANES), lambda j, i: (i, 0))
    if transposed:
        out_shape.append(jax.ShapeDtypeStruct((ncols, m), BF16))
        out_specs.append(pl.BlockSpec((tn, tm), lambda j, i: (j, i)))
    out = pl.pallas_call(
        functools.partial(_linear_kernel, scale=scale, sigmoid=sigmoid, has_res=res is not None,
                          has_norm=norm_g is not None, n_plain=len(out_dtypes), transposed=transposed,
                          slots=slots),
        out_shape=out_shape,
        grid=(ncols // tn, m // tm),
        in_specs=in_specs,
        out_specs=out_specs,
        compiler_params=_params(("parallel", "arbitrary")),
        name="linear",
    )(*args)
    return out


def _sb_consts(tk):
    r = lax.broadcasted_iota(jnp.int32, (tk, tk + LANES), 0)
    c = lax.broadcasted_iota(jnp.int32, (tk, tk + LANES), 1)
    return jnp.where((r > c) | (c >= tk), 1.0, 0.0).astype(BF16)


def _sb_chunk(qs, ks, vs, u2, acc_refs, c_refs, vis):
    tk = ks[0].shape[0]
    zs = [_dot_nt(q, k) for q, k in zip(qs, ks)]
    lks = []
    for z in zs:
        lk = -(jnp.maximum(z, 0.0) + jnp.log(1.0 + jnp.exp(-jnp.abs(z))))
        lks.append(lk if vis is None else jnp.where(vis, lk, 0.0))
    css = []
    for lk in lks:
        hi = lk.astype(BF16)
        lo = (lk - hi.astype(F32)).astype(BF16)
        css.append(_dot(hi, u2) + _dot(lo, u2))
    probs = []
    for z, lk, cs, c_ref in zip(zs, lks, css, c_refs):
        c = c_ref[...]
        a = jnp.exp(z + lk + cs[:, :tk] + c)
        probs.append((a if vis is None else jnp.where(vis, a, 0.0)).astype(BF16))
        c_ref[...] = c + cs[:, tk:]
    for a, v, acc_ref in zip(probs, vs, acc_refs):
        acc_ref[...] += _dot(a, v)


def _sb_prompt_kernel(q_ref, k_ref, v_ref, o_ref, acc_ref, c_ref, *, tq, dh):
    tk = SB_TK
    i = pl.program_id(1)
    nd = tq // tk
    heads = q_ref.shape[1] // dh
    acc_ref[...] = jnp.zeros_like(acc_ref)
    c_ref[...] = jnp.zeros_like(c_ref)
    u2 = _sb_consts(tk)
    row = lax.broadcasted_iota(jnp.int32, (tq, tk), 0)
    col = lax.broadcasted_iota(jnp.int32, (tq, tk), 1)

    def chunk(start, vis):
        sls = [slice(h * dh, (h + 1) * dh) for h in range(heads)]
        _sb_chunk([q_ref[:, sl] for sl in sls], [k_ref[pl.ds(start, tk), sl] for sl in sls],
                  [v_ref[pl.ds(start, tk), sl] for sl in sls], u2,
                  [acc_ref.at[h] for h in range(heads)], [c_ref.at[h] for h in range(heads)], vis)

    for d in range(nd - 1, -1, -1):
        chunk(pl.multiple_of(i * tq + d * tk, tk), (col + d * tk) < row)

    def live():
        return (jnp.max(c_ref[...]) > SB_EXIT).astype(jnp.int32)

    def cond(s):
        return jnp.logical_and(s[0] >= 0, s[1] > 0)

    def body(s):
        chunk(pl.multiple_of(s[0] * tk, tk), None)
        return s[0] - 1, live()

    lax.while_loop(cond, body, (i * nd - 1, live()))
    for h in range(heads):
        o_ref[:, h * dh:(h + 1) * dh] = acc_ref[h].astype(o_ref.dtype)


def _sb_prompt(q, k, v, n_heads, dh):
    t = q.shape[0]
    tq = min(t, 256)
    hs = SB_HEADS_PER_STEP
    assert n_heads % hs == 0
    w = hs * dh
    return pl.pallas_call(
        functools.partial(_sb_prompt_kernel, tq=tq, dh=dh),
        out_shape=jax.ShapeDtypeStruct((t, n_heads * dh), BF16),
        grid=(n_heads // hs, t // tq),
        in_specs=[pl.BlockSpec((tq, w), lambda h, i: (i, h)),
                  pl.BlockSpec((t, w), lambda h, i: (0, h)),
                  pl.BlockSpec((t, w), lambda h, i: (0, h))],
        out_specs=pl.BlockSpec((tq, w), lambda h, i: (i, h)),
        scratch_shapes=[pltpu.VMEM((hs, tq, dh), F32), pltpu.VMEM((hs, tq, LANES), F32)],
        compiler_params=_params(("parallel", "arbitrary")),
        name="sb_prompt",
    )(q, k, v)


def _softmax_steps(zs, vs, stats):
    alphas, ps = [], []
    for z, (m_ref, l_ref, _) in zip(zs, stats):
        m_old = m_ref[...]
        m_new = jnp.maximum(m_old, jnp.max(z, axis=1, keepdims=True))
        alpha = jnp.exp(m_old - m_new)
        p = jnp.exp(z - m_new)
        l_ref[...] = alpha * l_ref[...] + jnp.sum(p, axis=1, keepdims=True)
        m_ref[...] = m_new
        alphas.append(alpha)
        ps.append(p.astype(BF16))
    for alpha, p, v, (_, _, acc_ref) in zip(alphas, ps, vs, stats):
        acc_ref[...] = alpha * acc_ref[...] + _dot(p, v)


def _df_prompt_kernel(lam_ref, q_ref, k_ref, vt_ref, tiles_ref, g_ref, o_ref,
                      m_ref, l_ref, acc_ref, z_ref, *, dh, out_scale):
    blk = DF_BLK
    i = pl.program_id(1)
    m_ref[...] = jnp.full_like(m_ref, NEG)
    l_ref[...] = jnp.zeros_like(l_ref)
    acc_ref[...] = jnp.zeros_like(acc_ref)
    q = q_ref[...]

    def scores(first_blk, nblk):
        start = pl.multiple_of(first_blk * blk, blk)
        ks = k_ref[pl.ds(start, nblk * blk), :]
        return [_dot_nt(ks[:, c * dh:(c + 1) * dh], q[:, c * dh:(c + 1) * dh]) for c in range(2)]

    def step(first_blk, bias_t, nblk=1, zts=None):
        start = pl.multiple_of(first_blk * blk, blk)
        vt = vt_ref[:, pl.ds(start, nblk * blk)]
        if zts is None:
            zts = scores(first_blk, nblk)
        if bias_t is not None:
            zts = [zt + bias_t for zt in zts]
        alphas, ps = [], []
        for c, zt in enumerate(zts):
            m_old = m_ref[c]
            m_new = jnp.maximum(m_old, jnp.max(zt, axis=0, keepdims=True))
            alpha = jnp.exp2(m_old - m_new)
            p = jnp.exp2(zt - m_new)
            l_ref[c] = alpha * l_ref[c] + jnp.sum(p, axis=0, keepdims=True)
            m_ref[c] = m_new
            alphas.append(alpha)
            ps.append(p.astype(BF16))
        for c in range(2):
            acc_ref[c] = alphas[c] * acc_ref[c] + _dot(vt, ps[c])

    n_far = jnp.maximum(i - 1, 0)
    n_grp = n_far // DF_FAR_BLOCKS

    @pl.when(n_grp > 0)
    def _():
        for c, zt in enumerate(scores(0, DF_FAR_BLOCKS)):
            z_ref[0, c] = zt

    def far_body(j, carry):
        slot = lax.rem(j, 2)
        nxt = scores(jnp.minimum(j + 1, n_grp - 1) * DF_FAR_BLOCKS, DF_FAR_BLOCKS)
        step(j * DF_FAR_BLOCKS, None, DF_FAR_BLOCKS, zts=[z_ref[slot, 0], z_ref[slot, 1]])
        for c, zt in enumerate(nxt):
            z_ref[1 - slot, c] = zt
        return carry

    lax.fori_loop(0, n_grp, far_body, 0)

    def rest_body(kb, carry):
        step(kb, None)
        return carry

    lax.fori_loop(n_grp * DF_FAR_BLOCKS, n_far, rest_body, 0)

    @pl.when(i >= 1)
    def _():
        step(i - 1, tiles_ref[...], 2)

    @pl.when(i == 0)
    def _():
        step(0, tiles_ref[blk:, :])
    o_t = acc_ref[0] / l_ref[0] - lam_ref[0] * (acc_ref[1] / l_ref[1])
    o_ref[...] = (_rms(o_t.T, g_ref[...]) * out_scale).astype(o_ref.dtype)


def _df_prompt(q, k, v_t, tiles_t, lam, g, n_heads, dh, out_scale):
    t = q.shape[0]
    blk = DF_BLK
    dv = 2 * dh
    assert t % blk == 0
    grid_spec = pltpu.PrefetchScalarGridSpec(
        num_scalar_prefetch=1,
        grid=(n_heads, t // blk),
        in_specs=[pl.BlockSpec((blk, dv), lambda h, i, *_: (i, h)),
                  pl.BlockSpec((t, dv), lambda h, i, *_: (0, h)),
                  pl.BlockSpec((dv, t), lambda h, i, *_: (h, 0)),
                  pl.BlockSpec((None, 2 * blk, blk), lambda h, i, *_: (h, 0, 0)),
                  pl.BlockSpec((1, dv), lambda h, i, *_: (0, 0))],
        out_specs=pl.BlockSpec((blk, dv), lambda h, i, *_: (i, h)),
        scratch_shapes=[pltpu.VMEM((2, 1, blk), F32), pltpu.VMEM((2, 1, blk), F32),
                        pltpu.VMEM((2, dv, blk), F32),
                        pltpu.VMEM((2, 2, DF_FAR_BLOCKS * blk, blk), F32)],
    )
    return pl.pallas_call(
        functools.partial(_df_prompt_kernel, dh=dh, out_scale=out_scale),
        out_shape=jax.ShapeDtypeStruct((t, n_heads * dv), BF16),
        grid_spec=grid_spec,
        compiler_params=_params(("parallel", "arbitrary")),
        name="df_prompt",
    )(lam, q, k, v_t, tiles_t, g.reshape(1, dv))


def _df_decode_kernel(pt_ref, far_ref, lam_ref, q_ref, blast_ref, bnew_ref, g_ref, knew_ref, vnew_ref, kc_ref, vc_ref,
                      o_ref, kring, vring, sem, kbf, vbf, m_ref, l_ref, acc_ref, *, n_heads, dh, ts, out_scale):
    pg = PAGES_PER_STEP
    dv = 2 * dh
    s = pl.program_id(1)
    n_steps = pl.num_programs(1)
    last = s == n_steps - 1
    page = kbf.shape[0] // pg
    step = pl.program_id(0) * n_steps + s
    total = pl.num_programs(0) * n_steps

    def group_copies(grp, slot):
        cps = []
        for r in range(pg):
            pid = pt_ref[grp * pg + r]
            cps.append(pltpu.make_async_copy(kc_ref.at[pid], kring.at[slot, r], sem.at[0, slot]))
            cps.append(pltpu.make_async_copy(vc_ref.at[pid], vring.at[slot, r], sem.at[1, slot]))
        return cps

    @pl.when(step == 0)
    def _():
        for ahead in range(PAGE_BUFFERS - 1):
            @pl.when(ahead < total)
            def _():
                for cp in group_copies(ahead, ahead):
                    cp.start()

    nxt = step + PAGE_BUFFERS - 1

    @pl.when(nxt < total)
    def _():
        for cp in group_copies(nxt, lax.rem(nxt, PAGE_BUFFERS)):
            cp.start()

    slot = lax.rem(step, PAGE_BUFFERS)
    for cp in group_copies(step, slot):
        cp.wait()

    @pl.when(s == 0)
    def _():
        m_ref[...] = jnp.full_like(m_ref, NEG)
        l_ref[...] = jnp.zeros_like(l_ref)
        acc_ref[...] = jnp.zeros_like(acc_ref)

    for r in range(pg):
        rows_r = slice(r * page, (r + 1) * page)
        for o in range(2 * n_heads):
            kbf[rows_r, o * dh:(o + 1) * dh] = kring[slot, r, pl.ds(o, page, stride=2 * n_heads), :].astype(BF16)
            j, h = divmod(o, n_heads)
            vbf[rows_r, (2 * h + j) * dh:(2 * h + j + 1) * dh] = (
                vring[slot, r, pl.ds(o, page, stride=2 * n_heads), :].astype(BF16))
    heads = range(n_heads)
    stats = [(m_ref.at[h], l_ref.at[h], acc_ref.at[h]) for h in heads]
    zs = [_dot_nt(q_ref[h], kbf[:, h * dv:(h + 1) * dv]) + jnp.where(last, blast_ref[h], far_ref[h]) for h in heads]
    _softmax_steps(zs, [vbf[:, h * dv:(h + 1) * dv] for h in heads], stats)

    @pl.when(last)
    def _():
        pad = jnp.zeros((page - knew_ref.shape[0], dv), F32)
        kns = [jnp.concatenate([knew_ref[:, h * dv:(h + 1) * dv], pad], axis=0).astype(BF16) for h in heads]
        vns = [jnp.concatenate([vnew_ref[:, h * dv:(h + 1) * dv], pad], axis=0).astype(BF16) for h in heads]
        _softmax_steps([_dot_nt(q_ref[h], kns[h]) + bnew_ref[h] for h in heads], vns, stats)
        for h in heads:
            on = acc_ref[h] / l_ref[h]
            o = on - lam_ref[0] * pltpu.roll(on, on.shape[0] - ts, 0)
            o_ref[:, h * dv:(h + 1) * dv] = _rms(o, g_ref[...]) * out_scale


def _df_decode(page_table, far, lam, qbd, blast, bnew, g, knew, vnew, kcache, vcache, n_heads, dh, ts, out_scale):
    nb, n_pages = page_table.shape
    pg = PAGES_PER_STEP
    assert n_pages % pg == 0
    dv = 2 * dh
    width = n_heads * dv
    page = kcache.shape[1] * dh // width
    rows = qbd.shape[2]

    grid_spec = pltpu.PrefetchScalarGridSpec(
        num_scalar_prefetch=3,
        grid=(nb, n_pages // pg),
        in_specs=[pl.BlockSpec((None, n_heads, rows, dv), lambda b, s, *_: (b, 0, 0, 0)),
                  pl.BlockSpec((n_heads, rows, pg * page), lambda b, s, *_: (0, 0, 0)),
                  pl.BlockSpec((n_heads, rows, page), lambda b, s, *_: (0, 0, 0)),
                  pl.BlockSpec((1, dv), lambda b, s, *_: (0, 0)),
                  pl.BlockSpec((None, SUBLANES, width), lambda b, s, *_: (b, 0, 0)),
                  pl.BlockSpec((None, SUBLANES, width), lambda b, s, *_: (b, 0, 0)),
                  pl.BlockSpec(memory_space=pl.ANY),
                  pl.BlockSpec(memory_space=pl.ANY)],
        out_specs=pl.BlockSpec((None, rows, width), lambda b, s, *_: (b, 0, 0)),
        scratch_shapes=[pltpu.VMEM((PAGE_BUFFERS, pg) + kcache.shape[1:], F32),
                        pltpu.VMEM((PAGE_BUFFERS, pg) + vcache.shape[1:], F32),
                        pltpu.SemaphoreType.DMA((2, PAGE_BUFFERS)),
                        pltpu.VMEM((pg * page, width), BF16), pltpu.VMEM((pg * page, width), BF16),
                        pltpu.VMEM((n_heads, rows, 1), F32), pltpu.VMEM((n_heads, rows, 1), F32),
                        pltpu.VMEM((n_heads, rows, dv), F32)],
    )
    return pl.pallas_call(
        functools.partial(_df_decode_kernel, n_heads=n_heads, dh=dh, ts=ts, out_scale=out_scale),
        out_shape=jax.ShapeDtypeStruct((nb, rows, width), F32),
        grid_spec=grid_spec,
        compiler_params=_params(("arbitrary", "arbitrary")),
        name="df_decode",
    )(page_table.reshape(-1), far, lam, qbd, blast, bnew, g.reshape(1, dv), knew, vnew, kcache, vcache)


def _sb_decode_kernel(pt_ref, q_ref, knew_ref, vnew_ref, kc_ref, vc_ref, o_ref, kbuf, vbuf, sem, acc_ref, c_ref,
                      *, n_heads, dh, ts, n_pages):
    b = pl.program_id(0)
    page = kbuf.shape[1] // n_heads
    rows = q_ref.shape[1]
    u2 = _sb_consts(page)

    def page_copies(p, slot, seq=b):
        pid = pt_ref[seq * n_pages + p]
        return (pltpu.make_async_copy(kc_ref.at[pid], kbuf.at[slot], sem.at[0, slot]),
                pltpu.make_async_copy(vc_ref.at[pid], vbuf.at[slot], sem.at[1, slot]))

    def start(p, slot, seq=b):
        for cp in page_copies(p, slot, seq):
            cp.start()

    def wait(p, slot):
        for cp in page_copies(p, slot):
            cp.wait()

    @pl.when(b == 0)
    def _():
        start(n_pages - 1, (n_pages - 1) % 2)

    acc_ref[...] = jnp.zeros_like(acc_ref)
    c_ref[...] = jnp.zeros_like(c_ref)
    row = lax.broadcasted_iota(jnp.int32, (rows, page), 0)
    col = lax.broadcasted_iota(jnp.int32, (rows, page), 1)
    vis = col < jnp.minimum(row, ts)
    pad = jnp.zeros((page - knew_ref.shape[0], dh), F32)
    heads = range(n_heads)
    qs = [q_ref[h] for h in heads]
    acc_refs = [acc_ref.at[h] for h in heads]
    c_refs = [c_ref.at[h] for h in heads]
    _sb_chunk(qs,
              [jnp.concatenate([knew_ref[:, h * dh:(h + 1) * dh], pad], axis=0).astype(BF16) for h in heads],
              [jnp.concatenate([vnew_ref[:, h * dh:(h + 1) * dh], pad], axis=0).astype(BF16) for h in heads],
              u2, acc_refs, c_refs, vis)

    def live():
        return (jnp.max(c_ref[...]) > SB_EXIT).astype(jnp.int32)

    def cond(s):
        return jnp.logical_and(s[0] >= 0, s[1] > 0)

    def body(s):
        p = s[0]
        slot = lax.rem(p, 2)
        wait(p, slot)

        @pl.when(p >= 1)
        def _():
            start(p - 1, 1 - slot)

        _sb_chunk(qs,
                  [kbuf[slot, pl.ds(h, page, stride=n_heads), :].astype(BF16) for h in heads],
                  [vbuf[slot, pl.ds(h, page, stride=n_heads), :].astype(BF16) for h in heads],
                  u2, acc_refs, c_refs, None)
        return p - 1, live()

    p_end, _ = lax.while_loop(cond, body, (n_pages - 1, live()))

    @pl.when(p_end >= 0)
    def _():
        wait(p_end, lax.rem(p_end, 2))

    @pl.when(b + 1 < pl.num_programs(0))
    def _():
        start(n_pages - 1, (n_pages - 1) % 2, b + 1)

    for h in range(n_heads):
        o_ref[:, h * dh:(h + 1) * dh] = acc_ref[h]


def _sb_decode(page_table, q, knew, vnew, kcache, vcache, n_heads, dh, ts):
    nb, n_pages = page_table.shape
    width = n_heads * dh
    rows = q.shape[2]
    page_rows = kcache.shape[1]
    grid_spec = pltpu.PrefetchScalarGridSpec(
        num_scalar_prefetch=1,
        grid=(nb,),
        in_specs=[pl.BlockSpec((None, n_heads, rows, dh), lambda b, pt: (b, 0, 0, 0)),
                  pl.BlockSpec((None, SUBLANES, width), lambda b, pt: (b, 0, 0)),
                  pl.BlockSpec((None, SUBLANES, width), lambda b, pt: (b, 0, 0)),
                  pl.BlockSpec(memory_space=pl.ANY),
                  pl.BlockSpec(memory_space=pl.ANY)],
        out_specs=pl.BlockSpec((None, rows, width), lambda b, pt: (b, 0, 0)),
        scratch_shapes=[pltpu.VMEM((2, page_rows, dh), F32), pltpu.VMEM((2, page_rows, dh), F32),
                        pltpu.SemaphoreType.DMA((2, 2)),
                        pltpu.VMEM((n_heads, rows, dh), F32), pltpu.VMEM((n_heads, rows, LANES), F32)],
    )
    return pl.pallas_call(
        functools.partial(_sb_decode_kernel, n_heads=n_heads, dh=dh, ts=ts, n_pages=n_pages),
        out_shape=jax.ShapeDtypeStruct((nb, rows, width), F32),
        grid_spec=grid_spec,
        compiler_params=_params(("arbitrary",)),
        name="sb_decode",
    )(page_table.reshape(-1), q, knew, vnew, kcache, vcache)


def _merge_kernel(osb_ref, odf_ref, gsb_ref, gdf_ref, x_ref, wsb_ref, wdf_ref, wout_ref, g_ref, x1_ref, h_ref):
    y = (gsb_ref[...].astype(F32) * _dot(osb_ref[...], wsb_ref[...])
         + gdf_ref[...].astype(F32) * _dot(odf_ref[...], wdf_ref[...]))
    x1 = x_ref[...] + _dot(y.astype(BF16), wout_ref[...])
    x1_ref[...] = x1
    h_ref[...] = _rms(x1, g_ref[...]).astype(BF16)


def _merge(o_sb, o_df, gates, x, w_sb_o, w_diff_o, w_out, norm_g):
    m, d = x.shape
    w1 = o_sb.shape[1]
    tm = min(m, 256)
    const = lambda i: (0, 0)
    return pl.pallas_call(
        _merge_kernel,
        out_shape=[jax.ShapeDtypeStruct((m, d), F32), jax.ShapeDtypeStruct((m, d), BF16)],
        grid=(m // tm,),
        in_specs=[pl.BlockSpec((tm, w1), lambda i: (i, 0)),
                  pl.BlockSpec((tm, w1), lambda i: (i, 0)),
                  pl.BlockSpec((tm, d), lambda i: (i, 0)),
                  pl.BlockSpec((tm, d), lambda i: (i, 1)),
                  pl.BlockSpec((tm, d), lambda i: (i, 0)),
                  pl.BlockSpec((w1, d), const, pipeline_mode=pl.Buffered(1)),
                  pl.BlockSpec((w1, d), const, pipeline_mode=pl.Buffered(1)),
                  pl.BlockSpec((d, d), const, pipeline_mode=pl.Buffered(1)),
                  pl.BlockSpec((1, d), const)],
        out_specs=[pl.BlockSpec((tm, d), lambda i: (i, 0)), pl.BlockSpec((tm, d), lambda i: (i, 0))],
        compiler_params=_params(("parallel",)),
        name="merge",
    )(o_sb, o_df, gates, gates, x, w_sb_o, w_diff_o, w_out, norm_g.reshape(1, d))


def _mem_attn_kernel(q_ref, mk_ref, mv_ref, o_ref, *, n_heads, dh):
    interleaved = mk_ref.shape[1] == dh
    n_mem = mk_ref.shape[0] // n_heads if interleaved else mk_ref.shape[0]
    for h in range(n_heads):
        sl = slice(h * dh, (h + 1) * dh)
        if interleaved:
            mk = mk_ref[pl.ds(h, n_mem, stride=n_heads), :]
            mv = mv_ref[pl.ds(h, n_mem, stride=n_heads), :]
        else:
            mk, mv = mk_ref[:, sl], mv_ref[:, sl]
        z = _dot_nt(q_ref[:, sl], mk.astype(BF16))
        p = jnp.exp(z - jnp.max(z, axis=1, keepdims=True))
        o = _dot(p.astype(BF16), mv.astype(BF16)) / jnp.sum(p, axis=1, keepdims=True)
        o_ref[:, sl] = o.astype(o_ref.dtype)


def _mem_attn(q, mk, mv, n_heads, dh):
    nb, m, w = q.shape
    tm = min(m, 512)
    return pl.pallas_call(
        functools.partial(_mem_attn_kernel, n_heads=n_heads, dh=dh),
        out_shape=jax.ShapeDtypeStruct((nb, m, w), BF16),
        grid=(nb, m // tm),
        in_specs=[pl.BlockSpec((None, tm, w), lambda b, i: (b, i, 0)),
                  pl.BlockSpec((None,) + mk.shape[1:], lambda b, i: (b, 0, 0)),
                  pl.BlockSpec((None,) + mv.shape[1:], lambda b, i: (b, 0, 0))],
        out_specs=pl.BlockSpec((None, tm, w), lambda b, i: (b, i, 0)),
        compiler_params=_params(("parallel", "arbitrary")),
        name="mem_attn",
    )(q, mk, mv)


def _ffn_kernel(*refs, sample, ts, tail):
    if sample:
        h_ref, p1_ref, p2_ref = refs[:3]
        pos = 3
    else:
        h_ref, halo_ref = refs[:2]
        pos = 2
    (x_ref, wg_ref, wu_ref, wd_ref, cw_ref, cb_ref, gf_ref, y_ref, gout_ref, acc_ref) = refs[pos:]
    i = pl.program_id(0)
    f = pl.program_id(1)
    tm, tf = h_ref.shape[0], wg_ref.shape[1]

    @pl.when(f == 0)
    def _():
        acc_ref[...] = jnp.zeros_like(acc_ref)

    h = h_ref[...]
    g = _dot(h, wg_ref[...])
    u = _dot(h, wu_ref[...])
    g1 = pltpu.roll(g, 1, 0)
    g2 = pltpu.roll(g, 2, 0)
    if sample:
        gout_ref[...] = g
        t = lax.rem(lax.broadcasted_iota(jnp.int32, (tm, tf), 0), ts)
        g1 = jnp.where(t >= 1, g1, p1_ref[...])
        g2 = jnp.where(t >= 2, g2, p2_ref[...])
    else:
        gout_ref[...] = g[tm - tail:, :]
        gh = _dot(halo_ref[...], wg_ref[...]) * (i > 0).astype(F32)
        row = lax.broadcasted_iota(jnp.int32, (SUBLANES, tf), 0)
        top1 = jnp.where(row < 1, pltpu.roll(gh, 1, 0)[:SUBLANES], g1[:SUBLANES])
        top2 = jnp.where(row < 2, pltpu.roll(gh, 2, 0)[:SUBLANES], g2[:SUBLANES])
        g1 = jnp.concatenate([top1, g1[SUBLANES:]], axis=0)
        g2 = jnp.concatenate([top2, g2[SUBLANES:]], axis=0)
    cw = cw_ref[...]
    c = cb_ref[...] + cw[0:1] * g2 + cw[1:2] * g1 + cw[2:3] * g
    a = c / (1.0 + jnp.exp(-c)) * u
    acc_ref[...] += _dot(a.astype(BF16), wd_ref[...])

    @pl.when(f == pl.num_programs(1) - 1)
    def _():
        y_ref[...] = _rms(x_ref[...] + acc_ref[...], gf_ref[...])


def _ffn(h, x, wg, wu, wd, cw, cb, norm_f, *, prev=None, ts=1, tf=512):
    m, d = x.shape
    fp = wg.shape[1]
    sample = prev is not None
    tm = min(m, 512)
    tail = SUBLANES
    assert fp % tf == 0 and m % tm == 0
    row = lambda i, f: (i, 0)
    in_specs = [pl.BlockSpec((tm, d), row)]
    args = [h]
    if sample:
        in_specs += [pl.BlockSpec((tm, tf), lambda i, f: (i, f))] * 2
        args += list(prev)
        g_shape, g_spec = (m, fp), pl.BlockSpec((tm, tf), lambda i, f: (i, f))
    else:
        halo_blocks = tm // HALO
        in_specs.append(pl.BlockSpec((HALO, d), lambda i, f: (jnp.maximum(i * halo_blocks - 1, 0), 0)))
        args.append(h)
        g_shape, g_spec = (m // tm * tail, fp), pl.BlockSpec((tail, tf), lambda i, f: (i, f))
    in_specs += [pl.BlockSpec((tm, d), row),
                 pl.BlockSpec((d, tf), lambda i, f: (0, f)),
                 pl.BlockSpec((d, tf), lambda i, f: (0, f)),
                 pl.BlockSpec((tf, d), lambda i, f: (f, 0)),
                 pl.BlockSpec((CONV_W, tf), lambda i, f: (0, f)),
                 pl.BlockSpec((1, tf), lambda i, f: (0, f)),
                 pl.BlockSpec((1, d), lambda i, f: (0, 0))]
    args += [x, wg, wu, wd, cw, cb.reshape(1, fp), norm_f.reshape(1, d)]
    return pl.pallas_call(
        functools.partial(_ffn_kernel, sample=sample, ts=ts, tail=tail),
        out_shape=[jax.ShapeDtypeStruct((m, d), F32), jax.ShapeDtypeStruct(g_shape, F32)],
        grid=(m // tm, fp // tf),
        in_specs=in_specs,
        out_specs=[pl.BlockSpec((tm, d), row), g_spec],
        scratch_shapes=[pltpu.VMEM((tm, d), F32)],
        compiler_params=_params(("arbitrary", "arbitrary")),
        name="conv_ffn",
    )(*args)


def _t5_bias_by_distance(t5_bias, n):
    d = jnp.arange(n, dtype=jnp.int32)
    max_exact = NUM_BUCKETS // 2
    df = jnp.maximum(d, 1).astype(F32)
    large = max_exact + (jnp.log(df / max_exact) / math.log(MAX_DISTANCE / max_exact)
                         * (NUM_BUCKETS - max_exact)).astype(jnp.int32)
    large = jnp.minimum(large, NUM_BUCKETS - 1)
    bucket = jnp.where(d < max_exact, d, large)
    return t5_bias.astype(F32)[bucket].T


def _toeplitz_tile(bias_d, blk, off):
    n = 2 * blk
    j = jnp.arange(n, dtype=jnp.int32)
    dist = off - jnp.where(j < blk, j, j - n)
    e = jnp.where(dist >= 0, bias_d[:, jnp.clip(dist, 0, bias_d.shape[1] - 1)], NEG)
    skew = jnp.tile(e, (1, blk))[:, :blk * (n - 1)].reshape(-1, blk, n - 1)
    return skew[:, :, :blk]


def _bias_rows(bias_d, toks, first_dist, n_keys, n_valid):
    rows = []
    for tok in toks:
        top = first_dist + tok
        n_ok = min(n_valid, top + 1)
        ok = jnp.flip(bias_d[:, top - n_ok + 1:top + 1], axis=1)
        rows.append(jnp.pad(ok, ((0, 0), (0, n_keys - n_ok)), constant_values=NEG))
    return jnp.stack(rows, axis=1)


def _pad_rows(a, rows):
    pad = [(0, 0)] * a.ndim
    pad[-2] = (0, rows - a.shape[-2])
    return jnp.pad(a, pad)


def kernel(x_prompt, x_sample, mem_prompt, cache_sb_k, cache_sb_v, cache_diff_k, cache_diff_v, cache_mem_k, cache_mem_v, state_conv, page_table, norm_mix, w_in, w_sb_o, w_diff_o, w_out, diff_subln_g, lambda_q1, lambda_k1, lambda_q2, lambda_k2, t5_bias, norm_cross, norm_mem, w_mq, w_mk, w_mv, w_mo, norm_ffn, w_gate, w_up, conv_w, conv_b, w_down, norm_f):
    depth = w_in.shape[0]
    assert depth == 1 and x_prompt.shape[0] == 1
    _, t, d = x_prompt.shape
    nb, ts, _ = x_sample.shape
    n_pool, page, h_sb, dh = cache_sb_k.shape[1:]
    h_df = cache_diff_k.shape[3]
    dv = 2 * dh
    h_mem = cache_mem_k.shape[3]
    n_mem = mem_prompt.shape[1]
    d_ff = w_gate.shape[2]
    n_pages = page_table.shape[1]
    past = n_pages * page
    sb_w = h_sb * dh
    scale = dh ** -0.5
    lam_init = 0.8 - 0.6 * math.exp(-0.3 * 0)
    out_scale = 1.0 - lam_init
    assert 2 * ts <= QROWS and ts >= CONV_W - 1 and page == LANES and DF_BLK >= MAX_DISTANCE

    lam = (jnp.exp(jnp.sum(lambda_q1[0].astype(F32) * lambda_k1[0].astype(F32)))
           - jnp.exp(jnp.sum(lambda_q2[0].astype(F32) * lambda_k2[0].astype(F32))) + lam_init).reshape(1)

    bias_d = _t5_bias_by_distance(t5_bias, max(2 * DF_BLK, PAGES_PER_STEP * page + QROWS))
    far = bias_d[:, MAX_DISTANCE]
    tiles = jnp.concatenate([jnp.swapaxes(_toeplitz_tile(bias_d, DF_BLK, DF_BLK), 1, 2),
                             jnp.swapaxes(_toeplitz_tile(bias_d, DF_BLK, 0), 1, 2)], axis=1)

    tf = 512
    fp = ((d_ff + tf - 1) // tf) * tf
    w_in_b = w_in[0].astype(BF16)
    w_sb_o_b, w_diff_o_b, w_out_b = w_sb_o[0].astype(BF16), w_diff_o[0].astype(BF16), w_out[0].astype(BF16)
    w_mq_b, w_mk_b, w_mv_b, w_mo_b = (w[0].astype(BF16) for w in (w_mq, w_mk, w_mv, w_mo))
    wg_b = _cast_pad_bf16(w_gate[0], d, fp)
    wu_b = _cast_pad_bf16(w_up[0], d, fp)
    wd_b = _cast_pad_bf16(w_down[0], fp, d)
    cw_p = jnp.pad(conv_w[0], ((0, 0), (0, fp - d_ff)))
    cb_p = jnp.pad(conv_b[0], ((0, fp - d_ff),))

    def project(x2d, attn_copies):
        h = _rmsnorm_bf16(x2d, norm_mix[0])
        kv = [F32, BF16] if attn_copies else [F32]
        ident = list(range(sb_w // LANES)) if attn_copies else None
        v_slots = [2 * hh + j for j in range(2) for hh in range(h_df)] if attn_copies else None
        c = 0
        q_sb = _linear(h, w_in_b, c, sb_w, [BF16], scale=scale)[0]; c += sb_w
        k_sb = _linear(h, w_in_b, c, sb_w, kv, slots=ident); c += sb_w
        v_sb = _linear(h, w_in_b, c, sb_w, kv, slots=ident); c += sb_w
        q_df = _linear(h, w_in_b, c, h_df * dv, [BF16], scale=scale * LOG2E if attn_copies else scale)[0]
        c += h_df * dv
        k_df = _linear(h, w_in_b, c, h_df * dv, kv, slots=ident); c += h_df * dv
        v_df = _linear(h, w_in_b, c, h_df * dv, [F32], transposed=attn_copies, slots=v_slots); c += h_df * dv
        gates = _linear(h, w_in_b, c, 2 * d, [BF16], sigmoid=True, tn=2048)[0]
        return q_sb, k_sb, v_sb, q_df, k_df, v_df, gates

    def tail(x2d, o_sb, o_df, gates, mk, mv, rows_per_mem, ffn_prev, ffn_ts):
        x1, h2 = _merge(o_sb, o_df, gates, x2d, w_sb_o_b, w_diff_o_b, w_out_b, norm_cross[0])
        qm, = _linear(h2, w_mq_b, 0, h_mem * dh, [BF16], scale=scale)
        nbm = mk.shape[0]
        qm = qm.reshape(nbm, rows_per_mem, h_mem * dh)
        if rows_per_mem < QROWS:
            om = _mem_attn(_pad_rows(qm, QROWS), mk, mv, h_mem, dh)[:, :rows_per_mem]
        else:
            om = _mem_attn(qm, mk, mv, h_mem, dh)
        om = om.reshape(nbm * rows_per_mem, h_mem * dh)
        x2, h3 = _linear(om, w_mo_b, 0, d, [F32], res=x1, norm_g=norm_ffn[0])
        return _ffn(h3, x2, wg_b, wu_b, wd_b, cw_p, cb_p, norm_f, prev=ffn_prev, ts=ffn_ts, tf=tf)

    xp = x_prompt.reshape(t, d)
    q_sb, (k_sb, k_sb_b), (v_sb, v_sb_b), q_df, (k_df, k_df_b), (v_df, v_df_t), gates = project(xp, True)
    o_sb = _sb_prompt(q_sb, k_sb_b, v_sb_b, h_sb, dh)
    tiles_t = (tiles - far[:, None, None]) * LOG2E
    o_df = _df_prompt(q_df, k_df_b, v_df_t, tiles_t, lam, diff_subln_g[0], h_df, dh, out_scale)
    m_b = _rmsnorm_bf16(mem_prompt.reshape(n_mem, d), norm_mem[0])
    mk, = _linear(m_b, w_mk_b, 0, h_mem * dh, [F32])
    mv, = _linear(m_b, w_mv_b, 0, h_mem * dh, [F32])
    y_p, g_tail = tail(xp, o_sb, o_df, gates, mk[None], mv[None], t, None, 1)
    conv_p = g_tail[g_tail.shape[0] - (CONV_W - 1):, :d_ff]

    xs = x_sample.reshape(nb * ts, d)
    qs_sb, (ks_sb,), (vs_sb,), qs_df, (ks_df,), (vs_df,), gates_s = project(xs, False)
    q1 = _pad_rows(qs_sb.reshape(nb, ts, h_sb, dh).transpose(0, 2, 1, 3), QROWS)
    os_sb = _sb_decode(page_table, q1, _pad_rows(ks_sb.reshape(nb, ts, sb_w), SUBLANES),
                       _pad_rows(vs_sb.reshape(nb, ts, sb_w), SUBLANES),
                       cache_sb_k.reshape(n_pool, page * h_sb, dh), cache_sb_v.reshape(n_pool, page * h_sb, dh),
                       h_sb, dh, ts)[:, :ts].astype(BF16)
    q2 = qs_df.reshape(nb, ts, h_df, 2, dh).transpose(0, 2, 3, 1, 4)
    zq = jnp.zeros_like(q2[:, :, 0])
    qbd = jnp.concatenate([jnp.concatenate([q2[:, :, 0], zq], axis=-1),
                           jnp.concatenate([zq, q2[:, :, 1]], axis=-1)], axis=2)
    qbd = _pad_rows(qbd, QROWS)
    toks = list(range(ts)) * 2 + [0] * (QROWS - 2 * ts)
    n_last = PAGES_PER_STEP * page
    blast = _bias_rows(bias_d, toks, n_last, n_last, n_last)
    bnew = _bias_rows(bias_d, toks, 0, page, ts)
    kd2 = cache_diff_k.reshape(n_pool, page * h_df * 2, dh)
    vd2 = cache_diff_v.reshape(n_pool, page, h_df, 2, dh).transpose(0, 1, 3, 2, 4).reshape(n_pool, page * 2 * h_df, dh)
    os_df = _df_decode(page_table, far, lam, qbd, blast, bnew, diff_subln_g[0],
                       _pad_rows(ks_df.reshape(nb, ts, h_df * dv), SUBLANES),
                       _pad_rows(vs_df.reshape(nb, ts, h_df * dv), SUBLANES),
                       kd2, vd2, h_df, dh, ts, out_scale)[:, :ts].astype(BF16)
    st = jnp.pad(state_conv[0], ((0, 0), (0, 0), (0, fp - d_ff)))
    zero = jnp.zeros((nb, 1, fp), F32)
    p1 = jnp.concatenate([st[:, 1:2]] + [zero] * (ts - 1), axis=1).reshape(nb * ts, fp)
    p2 = jnp.concatenate([st[:, 0:1], st[:, 1:2]] + [zero] * (ts - 2), axis=1).reshape(nb * ts, fp)
    y_s, g_s = tail(xs, os_sb.reshape(nb * ts, sb_w), os_df.reshape(nb * ts, h_df * dv), gates_s,
                    cache_mem_k.reshape(nb, n_mem * h_mem, dh), cache_mem_v.reshape(nb, n_mem * h_mem, dh),
                    ts, (p1, p2), ts)
    gp = jnp.concatenate([state_conv[0], g_s[:, :d_ff].reshape(nb, ts, d_ff)], axis=1)
    conv_s = gp[:, ts:]

    return (y_p.reshape(1, t, d), y_s.reshape(nb, ts, d),
            k_sb.reshape(1, 1, t, h_sb, dh), v_sb.reshape(1, 1, t, h_sb, dh),
            k_df.reshape(1, 1, t, h_df, 2, dh),
            v_df.reshape(t, 2, h_df, dh).transpose(0, 2, 1, 3).reshape(1, 1, t, h_df, dv),
            mk.reshape(1, 1, n_mem, h_mem, dh), mv.reshape(1, 1, n_mem, h_mem, dh),
            conv_p.reshape(1, 1, CONV_W - 1, d_ff),
            ks_sb.reshape(1, nb, ts, h_sb, dh), vs_sb.reshape(1, nb, ts, h_sb, dh),
            ks_df.reshape(1, nb, ts, h_df, 2, dh), vs_df.reshape(1, nb, ts, h_df, dv),
            conv_s.reshape(1, nb, CONV_W - 1, d_ff))
```

```python
import functools
import math

import jax
import jax.numpy as jnp
from jax import lax
from jax.experimental import pallas as pl
from jax.experimental.pallas import tpu as pltpu

F32 = jnp.float32
BF16 = jnp.bfloat16
EPS = 1e-6
NUM_BUCKETS = 32
MAX_DISTANCE = 128
CONV_W = 3
LANES = 128
SUBLANES = 8
VMEM_LIMIT = 56 * 1024 * 1024
NEG = -1e30
LOG2E = 1.4426950408889634
SB_EXIT = -120.0
SB_TK = 128
SB_HEADS_PER_STEP = 4
DF_BLK = 256
DF_FAR_BLOCKS = 4
PAGES_PER_STEP = 8
PAGE_BUFFERS = 3
QROWS = 16
HALO = 16


def _params(sem):
    return pltpu.CompilerParams(dimension_semantics=sem, vmem_limit_bytes=VMEM_LIMIT)


def _dot(a, b):
    return jnp.dot(a, b, preferred_element_type=F32)


def _dot_nt(a, b):
    return lax.dot_general(a, b, (((1,), (1,)), ((), ())), preferred_element_type=F32)


def _rms(x, g):
    return x * lax.rsqrt(jnp.mean(x * x, axis=-1, keepdims=True) + EPS) * g


def _rms_kernel(x_ref, g_ref, o_ref):
    o_ref[...] = _rms(x_ref[...], g_ref[...]).astype(o_ref.dtype)


def _rmsnorm_bf16(x, g):
    m, d = x.shape
    tm = min(m, 512)
    return pl.pallas_call(
        _rms_kernel,
        out_shape=jax.ShapeDtypeStruct((m, d), BF16),
        grid=(m // tm,),
        in_specs=[pl.BlockSpec((tm, d), lambda i: (i, 0)), pl.BlockSpec((1, d), lambda i: (0, 0))],
        out_specs=pl.BlockSpec((tm, d), lambda i: (i, 0)),
        compiler_params=_params(("parallel",)),
        name="rmsnorm",
    )(x, g.reshape(1, d))


def _cast_pad_kernel(w_ref, o_ref, *, n_rows, n_cols):
    tr, tc = o_ref.shape
    r = pl.program_id(0) * tr + lax.broadcasted_iota(jnp.int32, (tr, tc), 0)
    c = pl.program_id(1) * tc + lax.broadcasted_iota(jnp.int32, (tr, tc), 1)
    w = jnp.where(r < n_rows, jnp.where(c < n_cols, w_ref[...], 0.0), 0.0)
    o_ref[...] = w.astype(o_ref.dtype)


def _cast_pad_bf16(w, rows, cols, tile=512):
    assert rows % tile == 0 and cols % tile == 0
    return pl.pallas_call(
        functools.partial(_cast_pad_kernel, n_rows=w.shape[0], n_cols=w.shape[1]),
        out_shape=jax.ShapeDtypeStruct((rows, cols), BF16),
        grid=(rows // tile, cols // tile),
        in_specs=[pl.BlockSpec((tile, tile), lambda i, j: (i, j))],
        out_specs=pl.BlockSpec((tile, tile), lambda i, j: (i, j)),
        compiler_params=_params(("parallel", "parallel")),
        name="cast_pad",
    )(w)


def _linear_kernel(*refs, scale, sigmoid, has_res, has_norm, n_plain, transposed, slots, pre_norm):
    x_ref, w_ref = refs[0], refs[1]
    pos = 2
    res_ref = g_ref = None
    if pre_norm:
        pre_g_ref = refs[pos]
        pos += 1
    if has_res:
        res_ref = refs[pos]
        pos += 1
    if has_norm:
        g_ref = refs[pos]
        pos += 1
    outs = refs[pos:]
    if pre_norm:
        x = _rms(x_ref[...], pre_g_ref[...]).astype(BF16)
        outs[-1][...] = x
        outs = outs[:-1]
    else:
        x = x_ref[...]
    y = _dot(x, w_ref[...])
    if scale is not None:
        y = y * scale
    if sigmoid:
        y = 1.0 / (1.0 + jnp.exp(-y))
    if has_res:
        y = y + res_ref[...]
    for k, o in enumerate(outs[:n_plain]):
        if k == 0 and slots is not None:
            for s, cb in enumerate(slots):
                o[pl.ds(s, y.shape[0], stride=len(slots)), :] = y[:, cb * LANES:(cb + 1) * LANES].astype(o.dtype)
        else:
            o[...] = y.astype(o.dtype)
    if has_norm:
        outs[n_plain][...] = _rms(y, g_ref[...]).astype(BF16)
    if transposed:
        outs[-1][...] = y.T.astype(BF16)


def _linear(x, w, col0, ncols, out_dtypes, *, scale=None, sigmoid=False, res=None, norm_g=None, tn=1024,
            transposed=False, slots=None, pre_norm_g=None):
    m, k = x.shape
    tm = min(m, 512)
    tn = min(tn, ncols)
    if norm_g is not None or slots is not None or pre_norm_g is not None:
        tn = ncols
    assert m % tm == 0 and ncols % tn == 0 and col0 % tn == 0
    off = col0 // tn
    in_specs = [pl.BlockSpec((tm, k), lambda j, i: (i, 0)),
                pl.BlockSpec((k, tn), lambda j, i: (0, j + off))]
    args = [x, w]
    if pre_norm_g is not None:
        in_specs.append(pl.BlockSpec((1, k), lambda j, i: (0, 0)))
        args.append(pre_norm_g.reshape(1, k))
    if res is not None:
        in_specs.append(pl.BlockSpec((tm, tn), lambda j, i: (i, j)))
        args.append(res)
    if norm_g is not None:
        in_specs.append(pl.BlockSpec((1, tn), lambda j, i: (0, 0)))
        args.append(norm_g.reshape(1, ncols))
    dts = list(out_dtypes) + ([BF16] if norm_g is not None else [])
    out_shape = [jax.ShapeDtypeStruct((m, ncols), dt) for dt in dts]
    out_specs = [pl.BlockSpec((tm, tn), lambda j, i: (i, j)) for _ in dts]
    if slots is not None:
        assert len(slots) * LANES == ncols
        out_shape[0] = jax.ShapeDtypeStruct((m * len(slots), LANES), dts[0])
        out_specs[0] = pl.BlockSpec((tm * len(slots), LANES), lambda j, i: (i, 0))
    if transposed:
        out_shape.append(jax.ShapeDtypeStruct((ncols, m), BF16))
        out_specs.append(pl.BlockSpec((tn, tm), lambda j, i: (j, i)))
    if pre_norm_g is not None:
        out_shape.append(jax.ShapeDtypeStruct((m, k), BF16))
        out_specs.append(pl.BlockSpec((tm, k), lambda j, i: (i, 0)))
    out = pl.pallas_call(
        functools.partial(_linear_kernel, scale=scale, sigmoid=sigmoid, has_res=res is not None,
                          has_norm=norm_g is not None, n_plain=len(out_dtypes), transposed=transposed,
                          slots=slots, pre_norm=pre_norm_g is not None),
        out_shape=out_shape,
        grid=(ncols // tn, m // tm),
        in_specs=in_specs,
        out_specs=out_specs,
        compiler_params=_params(("parallel", "arbitrary")),
        name="linear",
    )(*args)
    return out


def _sb_consts(tk):
    r = lax.broadcasted_iota(jnp.int32, (tk, tk + LANES), 0)
    c = lax.broadcasted_iota(jnp.int32, (tk, tk + LANES), 1)
    return jnp.where((r > c) | (c >= tk), 1.0, 0.0).astype(BF16)


def _sb_chunk(qs, ks, vs, u2, acc_refs, c_refs, vis):
    tk = ks[0].shape[0]
    zs = [_dot_nt(q, k) for q, k in zip(qs, ks)]
    lks = []
    for z in zs:
        lk = -(jnp.maximum(z, 0.0) + jnp.log(1.0 + jnp.exp(-jnp.abs(z))))
        lks.append(lk if vis is None else jnp.where(vis, lk, 0.0))
    css = []
    for lk in lks:
        hi = lk.astype(BF16)
        lo = (lk - hi.astype(F32)).astype(BF16)
        css.append(_dot(hi, u2) + _dot(lo, u2))
    probs = []
    for z, lk, cs, c_ref in zip(zs, lks, css, c_refs):
        c = c_ref[...]
        a = jnp.exp(z + lk + cs[:, :tk] + c)
        probs.append((a if vis is None else jnp.where(vis, a, 0.0)).astype(BF16))
        c_ref[...] = c + cs[:, tk:]
    for a, v, acc_ref in zip(probs, vs, acc_refs):
        acc_ref[...] += _dot(a, v)


def _sb_prompt_kernel(q_ref, k_ref, v_ref, o_ref, acc_ref, c_ref, *, tq, dh):
    tk = SB_TK
    i = pl.program_id(1)
    nd = tq // tk
    heads = q_ref.shape[1] // dh
    acc_ref[...] = jnp.zeros_like(acc_ref)
    c_ref[...] = jnp.zeros_like(c_ref)
    u2 = _sb_consts(tk)
    row = lax.broadcasted_iota(jnp.int32, (tq, tk), 0)
    col = lax.broadcasted_iota(jnp.int32, (tq, tk), 1)

    def chunk(start, vis):
        sls = [slice(h * dh, (h + 1) * dh) for h in range(heads)]
        _sb_chunk([q_ref[:, sl] for sl in sls], [k_ref[pl.ds(start, tk), sl] for sl in sls],
                  [v_ref[pl.ds(start, tk), sl] for sl in sls], u2,
                  [acc_ref.at[h] for h in range(heads)], [c_ref.at[h] for h in range(heads)], vis)

    for d in range(nd - 1, -1, -1):
        chunk(pl.multiple_of(i * tq + d * tk, tk), (col + d * tk) < row)

    def live():
        return (jnp.max(c_ref[...]) > SB_EXIT).astype(jnp.int32)

    def cond(s):
        return jnp.logical_and(s[0] >= 0, s[1] > 0)

    def body(s):
        chunk(pl.multiple_of(s[0] * tk, tk), None)
        return s[0] - 1, live()

    lax.while_loop(cond, body, (i * nd - 1, live()))
    for h in range(heads):
        o_ref[:, h * dh:(h + 1) * dh] = acc_ref[h].astype(o_ref.dtype)


def _sb_prompt(q, k, v, n_heads, dh):
    t = q.shape[0]
    tq = min(t, 256)
    hs = SB_HEADS_PER_STEP
    assert n_heads % hs == 0
    w = hs * dh
    return pl.pallas_call(
        functools.partial(_sb_prompt_kernel, tq=tq, dh=dh),
        out_shape=jax.ShapeDtypeStruct((t, n_heads * dh), BF16),
        grid=(n_heads // hs, t // tq),
        in_specs=[pl.BlockSpec((tq, w), lambda h, i: (i, h)),
                  pl.BlockSpec((t, w), lambda h, i: (0, h)),
                  pl.BlockSpec((t, w), lambda h, i: (0, h))],
        out_specs=pl.BlockSpec((tq, w), lambda h, i: (i, h)),
        scratch_shapes=[pltpu.VMEM((hs, tq, dh), F32), pltpu.VMEM((hs, tq, LANES), F32)],
        compiler_params=_params(("parallel", "arbitrary")),
        name="sb_prompt",
    )(q, k, v)


def _softmax_steps(zs, vs, stats):
    alphas, ps = [], []
    for z, (m_ref, l_ref, _) in zip(zs, stats):
        m_old = m_ref[...]
        m_new = jnp.maximum(m_old, jnp.max(z, axis=1, keepdims=True))
        alpha = jnp.exp(m_old - m_new)
        p = jnp.exp(z - m_new)
        l_ref[...] = alpha * l_ref[...] + jnp.sum(p, axis=1, keepdims=True)
        m_ref[...] = m_new
        alphas.append(alpha)
        ps.append(p.astype(BF16))
    for alpha, p, v, (_, _, acc_ref) in zip(alphas, ps, vs, stats):
        acc_ref[...] = alpha * acc_ref[...] + _dot(p, v)


def _df_prompt_kernel(lam_ref, q_ref, k_ref, vt_ref, tiles_ref, g_ref, o_ref,
                      m_ref, l_ref, acc_ref, z_ref, *, dh, out_scale):
    blk = DF_BLK
    i = pl.program_id(1)
    m_ref[...] = jnp.full_like(m_ref, NEG)
    l_ref[...] = jnp.zeros_like(l_ref)
    acc_ref[...] = jnp.zeros_like(acc_ref)
    q = q_ref[...]

    def scores(first_blk, nblk):
        start = pl.multiple_of(first_blk * blk, blk)
        ks = k_ref[pl.ds(start, nblk * blk), :]
        return [_dot_nt(ks[:, c * dh:(c + 1) * dh], q[:, c * dh:(c + 1) * dh]) for c in range(2)]

    def step(first_blk, bias_t, nblk=1, zts=None):
        start = pl.multiple_of(first_blk * blk, blk)
        vt = vt_ref[:, pl.ds(start, nblk * blk)]
        if zts is None:
            zts = scores(first_blk, nblk)
        if bias_t is not None:
            zts = [zt + bias_t for zt in zts]
        alphas, ps = [], []
        for c, zt in enumerate(zts):
            m_old = m_ref[c]
            m_new = jnp.maximum(m_old, jnp.max(zt, axis=0, keepdims=True))
            alpha = jnp.exp2(m_old - m_new)
            p = jnp.exp2(zt - m_new)
            l_ref[c] = alpha * l_ref[c] + jnp.sum(p, axis=0, keepdims=True)
            m_ref[c] = m_new
            alphas.append(alpha)
            ps.append(p.astype(BF16))
        for c in range(2):
            acc_ref[c] = alphas[c] * acc_ref[c] + _dot(vt, ps[c])

    n_far = jnp.maximum(i - 1, 0)
    n_grp = n_far // DF_FAR_BLOCKS

    @pl.when(n_grp > 0)
    def _():
        for c, zt in enumerate(scores(0, DF_FAR_BLOCKS)):
            z_ref[0, c] = zt

    def far_body(j, carry):
        slot = lax.rem(j, 2)
        nxt = scores(jnp.minimum(j + 1, n_grp - 1) * DF_FAR_BLOCKS, DF_FAR_BLOCKS)
        step(j * DF_FAR_BLOCKS, None, DF_FAR_BLOCKS, zts=[z_ref[slot, 0], z_ref[slot, 1]])
        for c, zt in enumerate(nxt):
            z_ref[1 - slot, c] = zt
        return carry

    lax.fori_loop(0, n_grp, far_body, 0)

    def rest_body(kb, carry):
        step(kb, None)
        return carry

    lax.fori_loop(n_grp * DF_FAR_BLOCKS, n_far, rest_body, 0)

    @pl.when(i >= 1)
    def _():
        step(i - 1, tiles_ref[...], 2)

    @pl.when(i == 0)
    def _():
        step(0, tiles_ref[blk:, :])
    o_t = acc_ref[0] / l_ref[0] - lam_ref[0] * (acc_ref[1] / l_ref[1])
    o_ref[...] = (_rms(o_t.T, g_ref[...]) * out_scale).astype(o_ref.dtype)


def _df_prompt(q, k, v_t, tiles_t, lam, g, n_heads, dh, out_scale):
    t = q.shape[0]
    blk = DF_BLK
    dv = 2 * dh
    assert t % blk == 0
    grid_spec = pltpu.PrefetchScalarGridSpec(
        num_scalar_prefetch=1,
        grid=(n_heads, t // blk),
        in_specs=[pl.BlockSpec((blk, dv), lambda h, i, *_: (i, h)),
                  pl.BlockSpec((t, dv), lambda h, i, *_: (0, h)),
                  pl.BlockSpec((dv, t), lambda h, i, *_: (h, 0)),
                  pl.BlockSpec((None, 2 * blk, blk), lambda h, i, *_: (h, 0, 0)),
                  pl.BlockSpec((1, dv), lambda h, i, *_: (0, 0))],
        out_specs=pl.BlockSpec((blk, dv), lambda h, i, *_: (i, h)),
        scratch_shapes=[pltpu.VMEM((2, 1, blk), F32), pltpu.VMEM((2, 1, blk), F32),
                        pltpu.VMEM((2, dv, blk), F32),
                        pltpu.VMEM((2, 2, DF_FAR_BLOCKS * blk, blk), F32)],
    )
    return pl.pallas_call(
        functools.partial(_df_prompt_kernel, dh=dh, out_scale=out_scale),
        out_shape=jax.ShapeDtypeStruct((t, n_heads * dv), BF16),
        grid_spec=grid_spec,
        compiler_params=_params(("parallel", "arbitrary")),
        name="df_prompt",
    )(lam, q, k, v_t, tiles_t, g.reshape(1, dv))


def _df_decode_kernel(pt_ref, far_ref, lam_ref, q_ref, blast_ref, bnew_ref, g_ref, knew_ref, vnew_ref, kc_ref, vc_ref,
                      o_ref, kring, vring, sem, kbf, vbf, m_ref, l_ref, acc_ref, *, n_heads, dh, ts, out_scale):
    pg = PAGES_PER_STEP
    dv = 2 * dh
    s = pl.program_id(1)
    n_steps = pl.num_programs(1)
    last = s == n_steps - 1
    page = kbf.shape[0] // pg
    step = pl.program_id(0) * n_steps + s
    total = pl.num_programs(0) * n_steps

    def group_copies(grp, slot):
        cps = []
        for r in range(pg):
            pid = pt_ref[grp * pg + r]
            cps.append(pltpu.make_async_copy(kc_ref.at[pid], kring.at[slot, r], sem.at[0, slot]))
            cps.append(pltpu.make_async_copy(vc_ref.at[pid], vring.at[slot, r], sem.at[1, slot]))
        return cps

    @pl.when(step == 0)
    def _():
        for ahead in range(PAGE_BUFFERS - 1):
            @pl.when(ahead < total)
            def _():
                for cp in group_copies(ahead, ahead):
                    cp.start()

    nxt = step + PAGE_BUFFERS - 1

    @pl.when(nxt < total)
    def _():
        for cp in group_copies(nxt, lax.rem(nxt, PAGE_BUFFERS)):
            cp.start()

    slot = lax.rem(step, PAGE_BUFFERS)
    for cp in group_copies(step, slot):
        cp.wait()

    @pl.when(s == 0)
    def _():
        m_ref[...] = jnp.full_like(m_ref, NEG)
        l_ref[...] = jnp.zeros_like(l_ref)
        acc_ref[...] = jnp.zeros_like(acc_ref)

    for r in range(pg):
        rows_r = slice(r * page, (r + 1) * page)
        for o in range(2 * n_heads):
            kbf[rows_r, o * dh:(o + 1) * dh] = kring[slot, r, pl.ds(o, page, stride=2 * n_heads), :].astype(BF16)
            j, h = divmod(o, n_heads)
            vbf[rows_r, (2 * h + j) * dh:(2 * h + j + 1) * dh] = (
                vring[slot, r, pl.ds(o, page, stride=2 * n_heads), :].astype(BF16))
    heads = range(n_heads)
    stats = [(m_ref.at[h], l_ref.at[h], acc_ref.at[h]) for h in heads]
    zs = [_dot_nt(q_ref[h], kbf[:, h * dv:(h + 1) * dv]) + jnp.where(last, blast_ref[h], far_ref[h]) for h in heads]
    _softmax_steps(zs, [vbf[:, h * dv:(h + 1) * dv] for h in heads], stats)

    @pl.when(last)
    def _():
        pad = jnp.zeros((page - knew_ref.shape[0], dv), F32)
        kns = [jnp.concatenate([knew_ref[:, h * dv:(h + 1) * dv], pad], axis=0).astype(BF16) for h in heads]
        vns = [jnp.concatenate([vnew_ref[:, h * dv:(h + 1) * dv], pad], axis=0).astype(BF16) for h in heads]
        _softmax_steps([_dot_nt(q_ref[h], kns[h]) + bnew_ref[h] for h in heads], vns, stats)
        for h in heads:
            on = acc_ref[h] / l_ref[h]
            o = on - lam_ref[0] * pltpu.roll(on, on.shape[0] - ts, 0)
            o_ref[:, h * dv:(h + 1) * dv] = _rms(o, g_ref[...]) * out_scale


def _df_decode(page_table, far, lam, qbd, blast, bnew, g, knew, vnew, kcache, vcache, n_heads, dh, ts, out_scale):
    nb, n_pages = page_table.shape
    pg = PAGES_PER_STEP
    assert n_pages % pg == 0
    dv = 2 * dh
    width = n_heads * dv
    page = kcache.shape[1] * dh // width
    rows = qbd.shape[2]

    grid_spec = pltpu.PrefetchScalarGridSpec(
        num_scalar_prefetch=3,
        grid=(nb, n_pages // pg),
        in_specs=[pl.BlockSpec((None, n_heads, rows, dv), lambda b, s, *_: (b, 0, 0, 0)),
                  pl.BlockSpec((n_heads, rows, pg * page), lambda b, s, *_: (0, 0, 0)),
                  pl.BlockSpec((n_heads, rows, page), lambda b, s, *_: (0, 0, 0)),
                  pl.BlockSpec((1, dv), lambda b, s, *_: (0, 0)),
                  pl.BlockSpec((None, SUBLANES, width), lambda b, s, *_: (b, 0, 0)),
                  pl.BlockSpec((None, SUBLANES, width), lambda b, s, *_: (b, 0, 0)),
                  pl.BlockSpec(memory_space=pl.ANY),
                  pl.BlockSpec(memory_space=pl.ANY)],
        out_specs=pl.BlockSpec((None, rows, width), lambda b, s, *_: (b, 0, 0)),
        scratch_shapes=[pltpu.VMEM((PAGE_BUFFERS, pg) + kcache.shape[1:], F32),
                        pltpu.VMEM((PAGE_BUFFERS, pg) + vcache.shape[1:], F32),
                        pltpu.SemaphoreType.DMA((2, PAGE_BUFFERS)),
                        pltpu.VMEM((pg * page, width), BF16), pltpu.VMEM((pg * page, width), BF16),
                        pltpu.VMEM((n_heads, rows, 1), F32), pltpu.VMEM((n_heads, rows, 1), F32),
                        pltpu.VMEM((n_heads, rows, dv), F32)],
    )
    return pl.pallas_call(
        functools.partial(_df_decode_kernel, n_heads=n_heads, dh=dh, ts=ts, out_scale=out_scale),
        out_shape=jax.ShapeDtypeStruct((nb, rows, width), F32),
        grid_spec=grid_spec,
        compiler_params=_params(("arbitrary", "arbitrary")),
        name="df_decode",
    )(page_table.reshape(-1), far, lam, qbd, blast, bnew, g.reshape(1, dv), knew, vnew, kcache, vcache)


def _sb_decode_kernel(pt_ref, q_ref, knew_ref, vnew_ref, kc_ref, vc_ref, o_ref, kbuf, vbuf, sem, acc_ref, c_ref,
                      *, n_heads, dh, ts, n_pages):
    b = pl.program_id(0)
    page = kbuf.shape[1] // n_heads
    rows = q_ref.shape[1]
    u2 = _sb_consts(page)

    def page_copies(p, slot, seq=b):
        pid = pt_ref[seq * n_pages + p]
        return (pltpu.make_async_copy(kc_ref.at[pid], kbuf.at[slot], sem.at[0, slot]),
                pltpu.make_async_copy(vc_ref.at[pid], vbuf.at[slot], sem.at[1, slot]))

    def start(p, slot, seq=b):
        for cp in page_copies(p, slot, seq):
            cp.start()

    def wait(p, slot):
        for cp in page_copies(p, slot):
            cp.wait()

    def start_newest(seq):
        for p in range(n_pages - 1, max(n_pages - 3, -1), -1):
            start(p, p % 2, seq)

    @pl.when(b == 0)
    def _():
        start_newest(b)

    acc_ref[...] = jnp.zeros_like(acc_ref)
    c_ref[...] = jnp.zeros_like(c_ref)
    row = lax.broadcasted_iota(jnp.int32, (rows, page), 0)
    col = lax.broadcasted_iota(jnp.int32, (rows, page), 1)
    vis = col < jnp.minimum(row, ts)
    pad = jnp.zeros((page - knew_ref.shape[0], dh), F32)
    heads = range(n_heads)
    qs = [q_ref[h] for h in heads]
    acc_refs = [acc_ref.at[h] for h in heads]
    c_refs = [c_ref.at[h] for h in heads]
    _sb_chunk(qs,
              [jnp.concatenate([knew_ref[:, h * dh:(h + 1) * dh], pad], axis=0).astype(BF16) for h in heads],
              [jnp.concatenate([vnew_ref[:, h * dh:(h + 1) * dh], pad], axis=0).astype(BF16) for h in heads],
              u2, acc_refs, c_refs, vis)

    def live():
        return (jnp.max(c_ref[...]) > SB_EXIT).astype(jnp.int32)

    def cond(s):
        return jnp.logical_and(s[0] >= 0, s[1] > 0)

    def body(s):
        p = s[0]
        slot = lax.rem(p, 2)
        wait(p, slot)
        _sb_chunk(qs,
                  [kbuf[slot, pl.ds(h, page, stride=n_heads), :].astype(BF16) for h in heads],
                  [vbuf[slot, pl.ds(h, page, stride=n_heads), :].astype(BF16) for h in heads],
                  u2, acc_refs, c_refs, None)

        @pl.when(p >= 2)
        def _():
            start(p - 2, slot)

        return p - 1, live()

    p_end, _ = lax.while_loop(cond, body, (n_pages - 1, live()))
    for back in range(2):
        @pl.when(p_end - back >= 0)
        def _():
            wait(p_end - back, lax.rem(p_end - back, 2))

    @pl.when(b + 1 < pl.num_programs(0))
    def _():
        start_newest(b + 1)

    for h in range(n_heads):
        o_ref[:, h * dh:(h + 1) * dh] = acc_ref[h]


def _sb_decode(page_table, q, knew, vnew, kcache, vcache, n_heads, dh, ts):
    nb, n_pages = page_table.shape
    width = n_heads * dh
    rows = q.shape[2]
    page_rows = kcache.shape[1]
    grid_spec = pltpu.PrefetchScalarGridSpec(
        num_scalar_prefetch=1,
        grid=(nb,),
        in_specs=[pl.BlockSpec((None, n_heads, rows, dh), lambda b, pt: (b, 0, 0, 0)),
                  pl.BlockSpec((None, SUBLANES, width), lambda b, pt: (b, 0, 0)),
                  pl.BlockSpec((None, SUBLANES, width), lambda b, pt: (b, 0, 0)),
                  pl.BlockSpec(memory_space=pl.ANY),
                  pl.BlockSpec(memory_space=pl.ANY)],
        out_specs=pl.BlockSpec((None, rows, width), lambda b, pt: (b, 0, 0)),
        scratch_shapes=[pltpu.VMEM((2, page_rows, dh), F32), pltpu.VMEM((2, page_rows, dh), F32),
                        pltpu.SemaphoreType.DMA((2, 2)),
                        pltpu.VMEM((n_heads, rows, dh), F32), pltpu.VMEM((n_heads, rows, LANES), F32)],
    )
    return pl.pallas_call(
        functools.partial(_sb_decode_kernel, n_heads=n_heads, dh=dh, ts=ts, n_pages=n_pages),
        out_shape=jax.ShapeDtypeStruct((nb, rows, width), F32),
        grid_spec=grid_spec,
        compiler_params=_params(("arbitrary",)),
        name="sb_decode",
    )(page_table.reshape(-1), q, knew, vnew, kcache, vcache)


def _merge_kernel(osb_ref, odf_ref, gsb_ref, gdf_ref, x_ref, wsb_ref, wdf_ref, wout_ref, g_ref, x1_ref, h_ref):
    y = (gsb_ref[...].astype(F32) * _dot(osb_ref[...], wsb_ref[...])
         + gdf_ref[...].astype(F32) * _dot(odf_ref[...], wdf_ref[...]))
    x1 = x_ref[...] + _dot(y.astype(BF16), wout_ref[...])
    x1_ref[...] = x1
    h_ref[...] = _rms(x1, g_ref[...]).astype(BF16)


def _merge(o_sb, o_df, gates, x, w_sb_o, w_diff_o, w_out, norm_g):
    m, d = x.shape
    w1 = o_sb.shape[1]
    tm = min(m, 256)
    const = lambda i: (0, 0)
    return pl.pallas_call(
        _merge_kernel,
        out_shape=[jax.ShapeDtypeStruct((m, d), F32), jax.ShapeDtypeStruct((m, d), BF16)],
        grid=(m // tm,),
        in_specs=[pl.BlockSpec((tm, w1), lambda i: (i, 0)),
                  pl.BlockSpec((tm, w1), lambda i: (i, 0)),
                  pl.BlockSpec((tm, d), lambda i: (i, 0)),
                  pl.BlockSpec((tm, d), lambda i: (i, 1)),
                  pl.BlockSpec((tm, d), lambda i: (i, 0)),
                  pl.BlockSpec((w1, d), const, pipeline_mode=pl.Buffered(1)),
                  pl.BlockSpec((w1, d), const, pipeline_mode=pl.Buffered(1)),
                  pl.BlockSpec((d, d), const, pipeline_mode=pl.Buffered(1)),
                  pl.BlockSpec((1, d), const)],
        out_specs=[pl.BlockSpec((tm, d), lambda i: (i, 0)), pl.BlockSpec((tm, d), lambda i: (i, 0))],
        compiler_params=_params(("parallel",)),
        name="merge",
    )(o_sb, o_df, gates, gates, x, w_sb_o, w_diff_o, w_out, norm_g.reshape(1, d))


def _mem_attn_kernel(q_ref, mk_ref, mv_ref, o_ref, *, n_heads, dh):
    interleaved = mk_ref.shape[1] == dh
    n_mem = mk_ref.shape[0] // n_heads if interleaved else mk_ref.shape[0]
    for h in range(n_heads):
        sl = slice(h * dh, (h + 1) * dh)
        if interleaved:
            mk = mk_ref[pl.ds(h, n_mem, stride=n_heads), :]
            mv = mv_ref[pl.ds(h, n_mem, stride=n_heads), :]
        else:
            mk, mv = mk_ref[:, sl], mv_ref[:, sl]
        z = _dot_nt(q_ref[:, sl], mk.astype(BF16))
        p = jnp.exp(z - jnp.max(z, axis=1, keepdims=True))
        o = _dot(p.astype(BF16), mv.astype(BF16)) / jnp.sum(p, axis=1, keepdims=True)
        o_ref[:, sl] = o.astype(o_ref.dtype)


def _mem_attn(q, mk, mv, n_heads, dh):
    nb, m, w = q.shape
    tm = min(m, 512)
    return pl.pallas_call(
        functools.partial(_mem_attn_kernel, n_heads=n_heads, dh=dh),
        out_shape=jax.ShapeDtypeStruct((nb, m, w), BF16),
        grid=(nb, m // tm),
        in_specs=[pl.BlockSpec((None, tm, w), lambda b, i: (b, i, 0)),
                  pl.BlockSpec((None,) + mk.shape[1:], lambda b, i: (b, 0, 0)),
                  pl.BlockSpec((None,) + mv.shape[1:], lambda b, i: (b, 0, 0))],
        out_specs=pl.BlockSpec((None, tm, w), lambda b, i: (b, i, 0)),
        compiler_params=_params(("parallel", "arbitrary")),
        name="mem_attn",
    )(q, mk, mv)


def _ffn_kernel(*refs, sample, ts, tail):
    if sample:
        h_ref, p1_ref, p2_ref = refs[:3]
        pos = 3
    else:
        h_ref, halo_ref = refs[:2]
        pos = 2
    (x_ref, wg_ref, wu_ref, wd_ref, cw_ref, cb_ref, gf_ref, y_ref, gout_ref, acc_ref) = refs[pos:]
    i = pl.program_id(0)
    f = pl.program_id(1)
    tm, tf = h_ref.shape[0], wg_ref.shape[1]

    @pl.when(f == 0)
    def _():
        acc_ref[...] = jnp.zeros_like(acc_ref)

    h = h_ref[...]
    g = _dot(h, wg_ref[...])
    u = _dot(h, wu_ref[...])
    g1 = pltpu.roll(g, 1, 0)
    g2 = pltpu.roll(g, 2, 0)
    if sample:
        gout_ref[...] = g
        t = lax.rem(lax.broadcasted_iota(jnp.int32, (tm, tf), 0), ts)
        g1 = jnp.where(t >= 1, g1, p1_ref[...])
        g2 = jnp.where(t >= 2, g2, p2_ref[...])
    else:
        gout_ref[...] = g[tm - tail:, :]
        gh = _dot(halo_ref[...], wg_ref[...]) * (i > 0).astype(F32)
        row = lax.broadcasted_iota(jnp.int32, (SUBLANES, tf), 0)
        top1 = jnp.where(row < 1, pltpu.roll(gh, 1, 0)[:SUBLANES], g1[:SUBLANES])
        top2 = jnp.where(row < 2, pltpu.roll(gh, 2, 0)[:SUBLANES], g2[:SUBLANES])
        g1 = jnp.concatenate([top1, g1[SUBLANES:]], axis=0)
        g2 = jnp.concatenate([top2, g2[SUBLANES:]], axis=0)
    cw = cw_ref[...]
    c = cb_ref[...] + cw[0:1] * g2 + cw[1:2] * g1 + cw[2:3] * g
    a = c / (1.0 + jnp.exp(-c)) * u
    acc_ref[...] += _dot(a.astype(BF16), wd_ref[...])

    @pl.when(f == pl.num_programs(1) - 1)
    def _():
        y_ref[...] = _rms(x_ref[...] + acc_ref[...], gf_ref[...])


def _ffn(h, x, wg, wu, wd, cw, cb, norm_f, *, prev=None, ts=1, tf=512):
    m, d = x.shape
    fp = wg.shape[1]
    sample = prev is not None
    tm = min(m, 512)
    tail = SUBLANES
    assert fp % tf == 0 and m % tm == 0
    row = lambda i, f: (i, 0)
    in_specs = [pl.BlockSpec((tm, d), row)]
    args = [h]
    if sample:
        in_specs += [pl.BlockSpec((tm, tf), lambda i, f: (i, f))] * 2
        args += list(prev)
        g_shape, g_spec = (m, fp), pl.BlockSpec((tm, tf), lambda i, f: (i, f))
    else:
        halo_blocks = tm // HALO
        in_specs.append(pl.BlockSpec((HALO, d), lambda i, f: (jnp.maximum(i * halo_blocks - 1, 0), 0)))
        args.append(h)
        g_shape, g_spec = (m // tm * tail, fp), pl.BlockSpec((tail, tf), lambda i, f: (i, f))
    in_specs += [pl.BlockSpec((tm, d), row),
                 pl.BlockSpec((d, tf), lambda i, f: (0, f)),
                 pl.BlockSpec((d, tf), lambda i, f: (0, f)),
                 pl.BlockSpec((tf, d), lambda i, f: (f, 0)),
                 pl.BlockSpec((CONV_W, tf), lambda i, f: (0, f)),
                 pl.BlockSpec((1, tf), lambda i, f: (0, f)),
                 pl.BlockSpec((1, d), lambda i, f: (0, 0))]
    args += [x, wg, wu, wd, cw, cb.reshape(1, fp), norm_f.reshape(1, d)]
    return pl.pallas_call(
        functools.partial(_ffn_kernel, sample=sample, ts=ts, tail=tail),
        out_shape=[jax.ShapeDtypeStruct((m, d), F32), jax.ShapeDtypeStruct(g_shape, F32)],
        grid=(m // tm, fp // tf),
        in_specs=in_specs,
        out_specs=[pl.BlockSpec((tm, d), row), g_spec],
        scratch_shapes=[pltpu.VMEM((tm, d), F32)],
        compiler_params=_params(("arbitrary", "arbitrary")),
        name="conv_ffn",
    )(*args)


def _t5_bias_by_distance(t5_bias, n):
    d = jnp.arange(n, dtype=jnp.int32)
    max_exact = NUM_BUCKETS // 2
    df = jnp.maximum(d, 1).astype(F32)
    large = max_exact + (jnp.log(df / max_exact) / math.log(MAX_DISTANCE / max_exact)
                         * (NUM_BUCKETS - max_exact)).astype(jnp.int32)
    large = jnp.minimum(large, NUM_BUCKETS - 1)
    bucket = jnp.where(d < max_exact, d, large)
    return t5_bias.astype(F32)[bucket].T


def _toeplitz_tile(bias_d, blk, off):
    n = 2 * blk
    j = jnp.arange(n, dtype=jnp.int32)
    dist = off - jnp.where(j < blk, j, j - n)
    e = jnp.where(dist >= 0, bias_d[:, jnp.clip(dist, 0, bias_d.shape[1] - 1)], NEG)
    skew = jnp.tile(e, (1, blk))[:, :blk * (n - 1)].reshape(-1, blk, n - 1)
    return skew[:, :, :blk]


def _bias_rows(bias_d, toks, first_dist, n_keys, n_valid):
    rows = []
    for tok in toks:
        top = first_dist + tok
        n_ok = min(n_valid, top + 1)
        ok = jnp.flip(bias_d[:, top - n_ok + 1:top + 1], axis=1)
        rows.append(jnp.pad(ok, ((0, 0), (0, n_keys - n_ok)), constant_values=NEG))
    return jnp.stack(rows, axis=1)


def _pad_rows(a, rows):
    pad = [(0, 0)] * a.ndim
    pad[-2] = (0, rows - a.shape[-2])
    return jnp.pad(a, pad)


def kernel(x_prompt, x_sample, mem_prompt, cache_sb_k, cache_sb_v, cache_diff_k, cache_diff_v, cache_mem_k, cache_mem_v, state_conv, page_table, norm_mix, w_in, w_sb_o, w_diff_o, w_out, diff_subln_g, lambda_q1, lambda_k1, lambda_q2, lambda_k2, t5_bias, norm_cross, norm_mem, w_mq, w_mk, w_mv, w_mo, norm_ffn, w_gate, w_up, conv_w, conv_b, w_down, norm_f):
    depth = w_in.shape[0]
    assert depth == 1 and x_prompt.shape[0] == 1
    _, t, d = x_prompt.shape
    nb, ts, _ = x_sample.shape
    n_pool, page, h_sb, dh = cache_sb_k.shape[1:]
    h_df = cache_diff_k.shape[3]
    dv = 2 * dh
    h_mem = cache_mem_k.shape[3]
    n_mem = mem_prompt.shape[1]
    d_ff = w_gate.shape[2]
    n_pages = page_table.shape[1]
    past = n_pages * page
    sb_w = h_sb * dh
    scale = dh ** -0.5
    lam_init = 0.8 - 0.6 * math.exp(-0.3 * 0)
    out_scale = 1.0 - lam_init
    assert 2 * ts <= QROWS and ts >= CONV_W - 1 and page == LANES and DF_BLK >= MAX_DISTANCE

    lam = (jnp.exp(jnp.sum(lambda_q1[0].astype(F32) * lambda_k1[0].astype(F32)))
           - jnp.exp(jnp.sum(lambda_q2[0].astype(F32) * lambda_k2[0].astype(F32))) + lam_init).reshape(1)

    bias_d = _t5_bias_by_distance(t5_bias, max(2 * DF_BLK, PAGES_PER_STEP * page + QROWS))
    far = bias_d[:, MAX_DISTANCE]
    tiles = jnp.concatenate([jnp.swapaxes(_toeplitz_tile(bias_d, DF_BLK, DF_BLK), 1, 2),
                             jnp.swapaxes(_toeplitz_tile(bias_d, DF_BLK, 0), 1, 2)], axis=1)

    tf = 512
    fp = ((d_ff + tf - 1) // tf) * tf
    w_in_b = w_in[0].astype(BF16)
    w_sb_o_b, w_diff_o_b, w_out_b = w_sb_o[0].astype(BF16), w_diff_o[0].astype(BF16), w_out[0].astype(BF16)
    w_mq_b, w_mk_b, w_mv_b, w_mo_b = (w[0].astype(BF16) for w in (w_mq, w_mk, w_mv, w_mo))
    wg_b = _cast_pad_bf16(w_gate[0], d, fp)
    wu_b = _cast_pad_bf16(w_up[0], d, fp)
    wd_b = _cast_pad_bf16(w_down[0], fp, d)
    cw_p = jnp.pad(conv_w[0], ((0, 0), (0, fp - d_ff)))
    cb_p = jnp.pad(conv_b[0], ((0, fp - d_ff),))

    def project(x2d, attn_copies):
        kv = [F32, BF16] if attn_copies else [F32]
        ident = list(range(sb_w // LANES)) if attn_copies else None
        v_slots = [2 * hh + j for j in range(2) for hh in range(h_df)] if attn_copies else None
        c = 0
        q_sb, h = _linear(x2d, w_in_b, c, sb_w, [BF16], scale=scale, pre_norm_g=norm_mix[0]); c += sb_w
        k_sb = _linear(h, w_in_b, c, sb_w, kv, slots=ident); c += sb_w
        v_sb = _linear(h, w_in_b, c, sb_w, kv, slots=ident); c += sb_w
        q_df = _linear(h, w_in_b, c, h_df * dv, [BF16], scale=scale * LOG2E if attn_copies else scale)[0]
        c += h_df * dv
        k_df = _linear(h, w_in_b, c, h_df * dv, kv, slots=ident); c += h_df * dv
        v_df = _linear(h, w_in_b, c, h_df * dv, [F32], transposed=attn_copies, slots=v_slots); c += h_df * dv
        gates = _linear(h, w_in_b, c, 2 * d, [BF16], sigmoid=True, tn=2048)[0]
        return q_sb, k_sb, v_sb, q_df, k_df, v_df, gates

    def tail(x2d, o_sb, o_df, gates, mk, mv, rows_per_mem, ffn_prev, ffn_ts):
        x1, h2 = _merge(o_sb, o_df, gates, x2d, w_sb_o_b, w_diff_o_b, w_out_b, norm_cross[0])
        qm, = _linear(h2, w_mq_b, 0, h_mem * dh, [BF16], scale=scale)
        nbm = mk.shape[0]
        qm = qm.reshape(nbm, rows_per_mem, h_mem * dh)
        if rows_per_mem < QROWS:
            om = _mem_attn(_pad_rows(qm, QROWS), mk, mv, h_mem, dh)[:, :rows_per_mem]
        else:
            om = _mem_attn(qm, mk, mv, h_mem, dh)
        om = om.reshape(nbm * rows_per_mem, h_mem * dh)
        x2, h3 = _linear(om, w_mo_b, 0, d, [F32], res=x1, norm_g=norm_ffn[0])
        return _ffn(h3, x2, wg_b, wu_b, wd_b, cw_p, cb_p, norm_f, prev=ffn_prev, ts=ffn_ts, tf=tf)

    xp = x_prompt.reshape(t, d)
    q_sb, (k_sb, k_sb_b), (v_sb, v_sb_b), q_df, (k_df, k_df_b), (v_df, v_df_t), gates = project(xp, True)
    o_sb = _sb_prompt(q_sb, k_sb_b, v_sb_b, h_sb, dh)
    tiles_t = (tiles - far[:, None, None]) * LOG2E
    o_df = _df_prompt(q_df, k_df_b, v_df_t, tiles_t, lam, diff_subln_g[0], h_df, dh, out_scale)
    m_b = _rmsnorm_bf16(mem_prompt.reshape(n_mem, d), norm_mem[0])
    mk, = _linear(m_b, w_mk_b, 0, h_mem * dh, [F32])
    mv, = _linear(m_b, w_mv_b, 0, h_mem * dh, [F32])
    y_p, g_tail = tail(xp, o_sb, o_df, gates, mk[None], mv[None], t, None, 1)
    conv_p = g_tail[g_tail.shape[0] - (CONV_W - 1):, :d_ff]

    xs = x_sample.reshape(nb * ts, d)
    qs_sb, (ks_sb,), (vs_sb,), qs_df, (ks_df,), (vs_df,), gates_s = project(xs, False)
    q1 = _pad_rows(qs_sb.reshape(nb, ts, h_sb, dh).transpose(0, 2, 1, 3), QROWS)
    os_sb = _sb_decode(page_table, q1, _pad_rows(ks_sb.reshape(nb, ts, sb_w), SUBLANES),
                       _pad_rows(vs_sb.reshape(nb, ts, sb_w), SUBLANES),
                       cache_sb_k.reshape(n_pool, page * h_sb, dh), cache_sb_v.reshape(n_pool, page * h_sb, dh),
                       h_sb, dh, ts)[:, :ts].astype(BF16)
    q2 = qs_df.reshape(nb, ts, h_df, 2, dh).transpose(0, 2, 3, 1, 4)
    zq = jnp.zeros_like(q2[:, :, 0])
    qbd = jnp.concatenate([jnp.concatenate([q2[:, :, 0], zq], axis=-1),
                           jnp.concatenate([zq, q2[:, :, 1]], axis=-1)], axis=2)
    qbd = _pad_rows(qbd, QROWS)
    toks = list(range(ts)) * 2 + [0] * (QROWS - 2 * ts)
    n_last = PAGES_PER_STEP * page
    blast = _bias_rows(bias_d, toks, n_last, n_last, n_last)
    bnew = _bias_rows(bias_d, toks, 0, page, ts)
    kd2 = cache_diff_k.reshape(n_pool, page * h_df * 2, dh)
    vd2 = cache_diff_v.reshape(n_pool, page, h_df, 2, dh).transpose(0, 1, 3, 2, 4).reshape(n_pool, page * 2 * h_df, dh)
    os_df = _df_decode(page_table, far, lam, qbd, blast, bnew, diff_subln_g[0],
                       _pad_rows(ks_df.reshape(nb, ts, h_df * dv), SUBLANES),
                       _pad_rows(vs_df.reshape(nb, ts, h_df * dv), SUBLANES),
                       kd2, vd2, h_df, dh, ts, out_scale)[:, :ts].astype(BF16)
    st = jnp.pad(state_conv[0], ((0, 0), (0, 0), (0, fp - d_ff)))
    zero = jnp.zeros((nb, 1, fp), F32)
    p1 = jnp.concatenate([st[:, 1:2]] + [zero] * (ts - 1), axis=1).reshape(nb * ts, fp)
    p2 = jnp.concatenate([st[:, 0:1], st[:, 1:2]] + [zero] * (ts - 2), axis=1).reshape(nb * ts, fp)
    y_s, g_s = tail(xs, os_sb.reshape(nb * ts, sb_w), os_df.reshape(nb * ts, h_df * dv), gates_s,
                    cache_mem_k.reshape(nb, n_mem * h_mem, dh), cache_mem_v.reshape(nb, n_mem * h_mem, dh),
                    ts, (p1, p2), ts)
    gp = jnp.concatenate([state_conv[0], g_s[:, :d_ff].reshape(nb, ts, d_ff)], axis=1)
    conv_s = gp[:, ts:]

    return (y_p.reshape(1, t, d), y_s.reshape(nb, ts, d),
            k_sb.reshape(1, 1, t, h_sb, dh), v_sb.reshape(1, 1, t, h_sb, dh),
            k_df.reshape(1, 1, t, h_df, 2, dh),
            v_df.reshape(t, 2, h_df, dh).transpose(0, 2, 1, 3).reshape(1, 1, t, h_df, dv),
            mk.reshape(1, 1, n_mem, h_mem, dh), mv.reshape(1, 1, n_mem, h_mem, dh),
            conv_p.reshape(1, 1, CONV_W - 1, d_ff),
            ks_sb.reshape(1, nb, ts, h_sb, dh), vs_sb.reshape(1, nb, ts, h_sb, dh),
            ks_df.reshape(1, nb, ts, h_df, 2, dh), vs_df.reshape(1, nb, ts, h_df, dv),
            conv_s.reshape(1, nb, CONV_W - 1, d_ff))
```
